```python
import jax, jax.numpy as jnp
from jax import lax
import numpy as np

D_MODEL = 4096
BATCH = 4
SEQ = 2048
DEPTH = 2
DEC_BATCH = 32
DEC_SEQ = 1
PAST_LEN = 16384
PAGE_SIZE = 128

N_MIXERS = 2
N_SWA_LAYERS = (DEPTH + 1) // 2
N_GLA_LAYERS = DEPTH // 2
EPS = 1e-6
SWA_HEAD_DIM = 128
SWA_Q_HEADS = D_MODEL // SWA_HEAD_DIM
SWA_KV_HEADS = 8
SWA_GROUP = SWA_Q_HEADS // SWA_KV_HEADS
SWA_Q_DIM = SWA_Q_HEADS * SWA_HEAD_DIM
SWA_KV_DIM = SWA_KV_HEADS * SWA_HEAD_DIM
WINDOW = 128
GLA_HEADS = 4
GLA_DK = D_MODEL // 2
GLA_DV = D_MODEL
GLA_DKH = GLA_DK // GLA_HEADS
GLA_DVH = GLA_DV // GLA_HEADS
GLA_GATE_RANK = 16
GLA_TAU = 16.0
GLA_CHUNK = 64
GLA_IN_DIM = 2 * GLA_DK + 2 * GLA_DV + GLA_GATE_RANK
MOE_GROUPS = 8
MOE_EXPERTS_PER_GROUP = 8
MOE_EXPERTS = MOE_GROUPS * MOE_EXPERTS_PER_GROUP
MOE_TOP_K = 2
MOE_D_FF = D_MODEL // 8
ROW_BLOCK = 128
SMALL_ROW_BLOCK = 8
PLE_DIM = 256

kernel_name = 'hybrid_swa_gla_hmoe_decode_step'


def _rmsnorm(x, g):
    xf = x.astype(jnp.float32)
    y = xf * lax.rsqrt(jnp.mean(xf * xf, axis=-1, keepdims=True) + EPS) * g.astype(jnp.float32)
    return y.astype(x.dtype)


def _swa_mixer(a, w_in, sinks, w_out, k_past, v_past):
    b, l, _ = a.shape
    qkv = a @ w_in
    q = qkv[..., :SWA_Q_DIM].reshape(b, l, SWA_KV_HEADS, SWA_GROUP, SWA_HEAD_DIM) * SWA_HEAD_DIM ** -0.5
    k = qkv[..., SWA_Q_DIM:SWA_Q_DIM + SWA_KV_DIM].reshape(b, l, SWA_KV_HEADS, SWA_HEAD_DIM)
    v = qkv[..., SWA_Q_DIM + SWA_KV_DIM:].reshape(b, l, SWA_KV_HEADS, SWA_HEAD_DIM)
    if k_past is None:
        nb = l // WINDOW
        qb = q.reshape(b, nb, WINDOW, SWA_KV_HEADS, SWA_GROUP, SWA_HEAD_DIM)

        def band(t):
            tb = t.reshape(b, nb, WINDOW, SWA_KV_HEADS, SWA_HEAD_DIM)
            prev = jnp.concatenate([jnp.zeros_like(tb[:, :1]), tb[:, :-1]], axis=1)
            return jnp.concatenate([prev, tb], axis=2)

        kb, vb = band(k), band(v)
        rel = WINDOW + jnp.arange(WINDOW)[:, None] - jnp.arange(2 * WINDOW)[None, :]
        in_win = (rel >= 0) & (rel < WINDOW)
        has_prev = (jnp.arange(nb) > 0)[:, None, None] | (jnp.arange(2 * WINDOW) >= WINDOW)[None, None, :]
        mask = in_win[None] & has_prev
        new_k, new_v = k[:, l - WINDOW:], v[:, l - WINDOW:]
    else:
        w_c = k_past.shape[1]
        k_all = jnp.concatenate([k_past.astype(k.dtype), k], axis=1)
        v_all = jnp.concatenate([v_past.astype(v.dtype), v], axis=1)
        qb, kb, vb = q[:, None], k_all[:, None], v_all[:, None]
        rel = w_c + jnp.arange(l)[:, None] - jnp.arange(w_c + l)[None, :]
        mask = ((rel >= 0) & (rel < WINDOW))[None]
        new_k, new_v = k_all[:, l:], v_all[:, l:]
    s = jnp.einsum('bnqkgd,bnskd->bnkgqs', qb, kb, preferred_element_type=jnp.float32)
    sk = sinks.astype(jnp.float32).reshape(SWA_KV_HEADS, SWA_GROUP)[:, :, None, None]
    s = jnp.where(mask[None, :, None, None], s, -jnp.inf)
    m = jnp.maximum(jnp.max(s, axis=-1, keepdims=True), sk)
    pexp = jnp.exp(s - m)
    prob = pexp / (jnp.sum(pexp, axis=-1, keepdims=True) + jnp.exp(sk - m))
    o = jnp.einsum('bnkgqs,bnskd->bnqkgd', prob.astype(vb.dtype), vb).reshape(b, l, SWA_Q_DIM)
    return o @ w_out, new_k, new_v


def _gla_scan(s0, q, k, v, log_a):
    b, l, h, _ = q.shape
    dv = v.shape[-1]
    c = GLA_CHUNK if l % GLA_CHUNK == 0 else l
    nc = l // c

    def to_chunks(t):
        return t.reshape(b, nc, c, h, t.shape[-1]).transpose(1, 0, 3, 2, 4)

    causal = jnp.tril(jnp.ones((c, c), dtype=bool))[:, :, None]

    def step(s, inp):
        qc, kc, vc, gc = inp
        cum = jnp.cumsum(gc, axis=2)
        o_inter = jnp.einsum('bhcd,bhde->bhce', qc * jnp.exp(cum), s)
        diff = cum[:, :, :, None, :] - cum[:, :, None, :, :]
        decay = jnp.exp(jnp.where(causal, diff, -jnp.inf))
        att = jnp.einsum('bhijd,bhjd->bhij', qc[:, :, :, None, :] * decay, kc)
        o_intra = jnp.einsum('bhij,bhje->bhie', att, vc)
        last = cum[:, :, -1, :]
        s_new = jnp.exp(last)[..., None] * s + jnp.einsum(
            'bhcd,bhce->bhde', kc * jnp.exp(last[:, :, None, :] - cum), vc)
        return s_new, o_inter + o_intra

    s_fin, o = lax.scan(step, s0, (to_chunks(q), to_chunks(k), to_chunks(v), to_chunks(log_a)))
    o = o.transpose(1, 0, 3, 2, 4).reshape(b, l, h, dv)
    return o, s_fin


def _gla_mixer(a, w_in, w_gate_up, b_gate, norm_g, w_out, s0):
    b, l, _ = a.shape
    proj = a @ w_in
    q, k, v, r, gl = jnp.split(proj, [GLA_DK, 2 * GLA_DK, 2 * GLA_DK + GLA_DV, 2 * GLA_DK + 2 * GLA_DV], axis=-1)
    log_a = jax.nn.log_sigmoid((gl @ w_gate_up + b_gate).astype(jnp.float32)) / GLA_TAU
    qf = q.reshape(b, l, GLA_HEADS, GLA_DKH).astype(jnp.float32) * GLA_DKH ** -0.5
    kf = k.reshape(b, l, GLA_HEADS, GLA_DKH).astype(jnp.float32)
    vf = v.reshape(b, l, GLA_HEADS, GLA_DVH).astype(jnp.float32)
    o, s_fin = _gla_scan(s0, qf, kf, vf, log_a.reshape(b, l, GLA_HEADS, GLA_DKH))
    o = _rmsnorm(o, norm_g).reshape(b, l, GLA_DV).astype(a.dtype) * jax.nn.silu(r)
    return o @ w_out, s_fin


def _grouped_swiglu(rows, eid, w_gate, w_up, w_down):
    n_rows, d = rows.shape
    n_exp = w_gate.shape[0]
    rb = ROW_BLOCK if n_rows >= n_exp * ROW_BLOCK else SMALL_ROW_BLOCK
    n_blk = -(-(n_rows + n_exp * (rb - 1)) // rb)
    counts = jnp.bincount(eid, length=n_exp).astype(jnp.int32)
    padded = ((counts + rb - 1) // rb) * rb
    pad_end = jnp.cumsum(padded)
    start = jnp.cumsum(counts) - counts
    dest = (pad_end - padded)[eid] + jnp.arange(n_rows, dtype=jnp.int32) - start[eid]
    buf = jnp.zeros((n_blk * rb, d), rows.dtype).at[dest].set(rows)
    blk_expert = jnp.minimum(jnp.searchsorted(pad_end, jnp.arange(n_blk, dtype=jnp.int32) * rb, side='right'), n_exp - 1)

    def one_block(args):
        xb, e = args
        return (jax.nn.silu(xb @ w_gate[e]) * (xb @ w_up[e])) @ w_down[e]

    yb = lax.map(one_block, (buf.reshape(n_blk, rb, d), blk_expert))
    return yb.reshape(n_blk * rb, d)[dest]


def _hier_moe(x, w_grp, b_grp, w_exp, b_exp, w_gate, w_up, w_down):
    shp = x.shape
    xt = x.reshape(-1, shp[-1])
    t = xt.shape[0]
    g_prob = jax.nn.softmax((xt @ w_grp).astype(jnp.float32) + b_grp.astype(jnp.float32), axis=-1)
    g_sel = jnp.argmax(g_prob, axis=-1)
    p_g = jnp.take_along_axis(g_prob, g_sel[:, None], axis=-1)
    e_logits = ((xt @ w_exp).astype(jnp.float32) + b_exp.astype(jnp.float32)).reshape(t, MOE_GROUPS, MOE_EXPERTS_PER_GROUP)
    e_sel_logits = jnp.take_along_axis(e_logits, g_sel[:, None, None], axis=1)[:, 0]
    top_v, top_i = lax.top_k(e_sel_logits, MOE_TOP_K)
    gates = jax.nn.softmax(top_v, axis=-1) * p_g
    eid = (g_sel[:, None] * MOE_EXPERTS_PER_GROUP + top_i).reshape(-1).astype(jnp.int32)
    order = jnp.argsort(eid)
    tok = order // MOE_TOP_K
    y_sorted = _grouped_swiglu(xt[tok], eid[order], w_gate, w_up, w_down)
    wts = gates.reshape(-1)[order].astype(y_sorted.dtype)
    out = jax.ops.segment_sum(y_sorted * wts[:, None], tok, num_segments=t)
    return out.reshape(shp).astype(x.dtype)


def _trunk(x, p, k_cache, v_cache, s_cache, w):
    h = x
    new_k, new_v, new_s = [], [], []
    for i in range(DEPTH):
        j = i // N_MIXERS
        a = _rmsnorm(h, w['ln_mix'][i])
        if i % N_MIXERS == 0:
            mix, nk, nv = _swa_mixer(a, w['swa_w_in'][j], w['swa_sinks'][j], w['swa_w_out'][j],
                                     None if k_cache is None else k_cache[j],
                                     None if v_cache is None else v_cache[j])
            new_k.append(nk)
            new_v.append(nv)
        else:
            if s_cache is None:
                s0 = jnp.zeros((x.shape[0], GLA_HEADS, GLA_DKH, GLA_DVH), jnp.float32)
            else:
                s0 = s_cache[j].astype(jnp.float32)
            mix, ns = _gla_mixer(a, w['gla_w_in'][j], w['gla_w_gate_up'][j], w['gla_b_gate'][j],
                                 w['gla_norm'][j], w['gla_w_out'][j], s0)
            new_s.append(ns.astype(x.dtype))
        h = h + mix
        h = h + _hier_moe(_rmsnorm(h, w['ln_ffn'][i]), w['moe_w_group'][i], w['moe_b_group'][i],
                          w['moe_w_expert'][i], w['moe_b_expert'][i], w['moe_w_gate'][i],
                          w['moe_w_up'][i], w['moe_w_down'][i])
        gate = jax.nn.sigmoid(_rmsnorm(h, w['ln_ple'][i]) @ w['ple_w_gate'][i])
        h = h + (p[i].astype(h.dtype) @ w['ple_w_proj'][i]) * gate
    return _rmsnorm(h, w['ln_final']), jnp.stack(new_k), jnp.stack(new_v), jnp.stack(new_s)


def setup_inputs(seed: int = 0) -> dict:
    key = jax.random.key(seed)
    ks = jax.random.split(key, 32)
    f32 = jnp.float32
    d = D_MODEL
    w_c = min(WINDOW, PAST_LEN)

    def nrm(k, shape, scale):
        return jax.random.normal(k, shape, f32) * scale

    return {
        'x_prompt': nrm(ks[0], (BATCH, SEQ, d), 1.0),
        'x_sample': nrm(ks[1], (DEC_BATCH, DEC_SEQ, d), 1.0),
        'cache_swa_k': nrm(ks[2], (N_SWA_LAYERS, DEC_BATCH, w_c, SWA_KV_HEADS, SWA_HEAD_DIM), 1.0),
        'cache_swa_v': nrm(ks[3], (N_SWA_LAYERS, DEC_BATCH, w_c, SWA_KV_HEADS, SWA_HEAD_DIM), 1.0),
        'state_gla': nrm(ks[4], (N_GLA_LAYERS, DEC_BATCH, GLA_HEADS, GLA_DKH, GLA_DVH), 0.1),
        'p_prompt': nrm(ks[5], (DEPTH, BATCH, SEQ, PLE_DIM), 1.0),
        'p_sample': nrm(ks[6], (DEPTH, DEC_BATCH, DEC_SEQ, PLE_DIM), 1.0),
        'ln_mix': 1.0 + nrm(ks[7], (DEPTH, d), 0.05),
        'ln_ffn': 1.0 + nrm(ks[8], (DEPTH, d), 0.05),
        'ln_ple': 1.0 + nrm(ks[9], (DEPTH, d), 0.05),
        'ln_final': 1.0 + nrm(ks[10], (d,), 0.05),
        'swa_w_in': nrm(ks[11], (N_SWA_LAYERS, d, SWA_Q_DIM + 2 * SWA_KV_DIM), d ** -0.5),
        'swa_sinks': nrm(ks[12], (N_SWA_LAYERS, SWA_Q_HEADS), 1.0),
        'swa_w_out': nrm(ks[13], (N_SWA_LAYERS, SWA_Q_DIM, d), SWA_Q_DIM ** -0.5),
        'gla_w_in': nrm(ks[14], (N_GLA_LAYERS, d, GLA_IN_DIM), d ** -0.5),
        'gla_w_gate_up': nrm(ks[15], (N_GLA_LAYERS, GLA_GATE_RANK, GLA_DK), GLA_GATE_RANK ** -0.5),
        'gla_b_gate': nrm(ks[16], (N_GLA_LAYERS, GLA_DK), 0.5),
        'gla_norm': 1.0 + nrm(ks[17], (N_GLA_LAYERS, GLA_DVH), 0.05),
        'gla_w_out': nrm(ks[18], (N_GLA_LAYERS, GLA_DV, d), GLA_DV ** -0.5),
        'moe_w_group': nrm(ks[19], (DEPTH, d, MOE_GROUPS), d ** -0.5),
        'moe_b_group': nrm(ks[20], (DEPTH, MOE_GROUPS), 0.01),
        'moe_w_expert': nrm(ks[21], (DEPTH, d, MOE_EXPERTS), d ** -0.5),
        'moe_b_expert': nrm(ks[22], (DEPTH, MOE_EXPERTS), 0.01),
        'moe_w_gate': nrm(ks[23], (DEPTH, MOE_EXPERTS, d, MOE_D_FF), d ** -0.5),
        'moe_w_up': nrm(ks[24], (DEPTH, MOE_EXPERTS, d, MOE_D_FF), d ** -0.5),
        'moe_w_down': nrm(ks[25], (DEPTH, MOE_EXPERTS, MOE_D_FF, d), MOE_D_FF ** -0.5),
        'ple_w_proj': nrm(ks[26], (DEPTH, PLE_DIM, d), PLE_DIM ** -0.5),
        'ple_w_gate': nrm(ks[27], (DEPTH, d, d), d ** -0.5),
    }


def reference(x_prompt, x_sample, cache_swa_k, cache_swa_v, state_gla, p_prompt, p_sample,
              ln_mix, ln_ffn, ln_ple, ln_final, swa_w_in, swa_sinks, swa_w_out,
              gla_w_in, gla_w_gate_up, gla_b_gate, gla_norm, gla_w_out,
              moe_w_group, moe_b_group, moe_w_expert, moe_b_expert, moe_w_gate, moe_w_up, moe_w_down,
              ple_w_proj, ple_w_gate):
    w = {
        'ln_mix': ln_mix, 'ln_ffn': ln_ffn, 'ln_ple': ln_ple, 'ln_final': ln_final,
        'swa_w_in': swa_w_in, 'swa_sinks': swa_sinks, 'swa_w_out': swa_w_out,
        'gla_w_in': gla_w_in, 'gla_w_gate_up': gla_w_gate_up, 'gla_b_gate': gla_b_gate,
        'gla_norm': gla_norm, 'gla_w_out': gla_w_out,
        'moe_w_group': moe_w_group, 'moe_b_group': moe_b_group, 'moe_w_expert': moe_w_expert,
        'moe_b_expert': moe_b_expert, 'moe_w_gate': moe_w_gate, 'moe_w_up': moe_w_up,
        'moe_w_down': moe_w_down, 'ple_w_proj': ple_w_proj, 'ple_w_gate': ple_w_gate,
    }
    y_prompt, k_prompt, v_prompt, s_prompt = _trunk(x_prompt, p_prompt, None, None, None, w)
    y_sample, k_sample, v_sample, s_sample = _trunk(x_sample, p_sample, cache_swa_k, cache_swa_v, state_gla, w)
    return (y_prompt, y_sample, k_prompt, v_prompt, s_prompt, k_sample, v_sample, s_sample)
```

```python
import functools

import jax
import jax.numpy as jnp
from jax import lax
from jax.experimental import pallas as pl
from jax.experimental.pallas import tpu as pltpu

F32 = jnp.float32
BF16 = jnp.bfloat16

EPS = 1e-6
WINDOW = 128
SWA_HEAD_DIM = 128
GLA_HEADS = 4
GLA_GATE_RANK = 16
GLA_TAU = 16.0
GLA_CHUNK = 128
GLA_DIAG = 8
MOE_GROUPS = 8
MOE_EXPERTS_PER_GROUP = 8
MOE_EXPERTS = MOE_GROUPS * MOE_EXPERTS_PER_GROUP
MOE_TOP_K = 2
ROUTER_LANES = 128

V7X_VMEM_LIMIT_BYTES = 60 * 1024 * 1024
ROW_TILE = 512
COL_TILE = 512
MOE_ROW_BLOCK = 512
MOE_SUB_BLOCK = 128
MOE_FF_TILE = 256
CAST_ROWS = 256


def _params(*sem):
    return pltpu.CompilerParams(dimension_semantics=sem, vmem_limit_bytes=V7X_VMEM_LIMIT_BYTES)


def _sigmoid(x):
    return 1.0 / (1.0 + jnp.exp(-x))


def _cast_rows(src_ref, dst_ref):
    rows = src_ref.shape[0]
    step = min(CAST_ROWS, rows)

    def body(i, c):
        r = pl.multiple_of(i * step, step)
        dst_ref[pl.ds(r, step), :] = src_ref[pl.ds(r, step), :].astype(BF16)
        return c

    lax.fori_loop(0, rows // step, body, 0)


def _rms(x, g):
    ms = jnp.mean(x * x, axis=-1, keepdims=True)
    return x * lax.rsqrt(ms + EPS) * g


def _norm_kernel(h_ref, g_ref, o_ref):
    o_ref[...] = _rms(h_ref[...], g_ref[...]).astype(o_ref.dtype)


def _norm_proj_kernel(h_ref, g_ref, w_ref, o_ref, s_ref):
    y = _rms(h_ref[...], g_ref[...]).astype(BF16)
    o_ref[...] = y
    s_ref[...] = jnp.dot(y, w_ref[...].astype(BF16), preferred_element_type=F32)


def _route(logits):
    lane = lax.broadcasted_iota(jnp.int32, logits.shape, 1)
    neg = jnp.float32(-jnp.inf)
    is_g = lane < MOE_GROUPS
    lg = jnp.where(is_g, logits, neg)
    mg = jnp.max(lg, axis=-1, keepdims=True)
    g_sel = jnp.min(jnp.where(lg == mg, lane, ROUTER_LANES), axis=-1, keepdims=True)
    p_g = 1.0 / jnp.sum(jnp.where(is_g, jnp.exp(lg - mg), 0.0), axis=-1, keepdims=True)
    lo = MOE_GROUPS + g_sel * MOE_EXPERTS_PER_GROUP
    in_grp = (lane >= lo) & (lane < lo + MOE_EXPERTS_PER_GROUP)
    le = jnp.where(in_grp, logits, neg)
    v0 = jnp.max(le, axis=-1, keepdims=True)
    i0 = jnp.min(jnp.where(le == v0, lane, ROUTER_LANES), axis=-1, keepdims=True)
    le1 = jnp.where(lane == i0, neg, le)
    v1 = jnp.max(le1, axis=-1, keepdims=True)
    i1 = jnp.min(jnp.where(le1 == v1, lane, ROUTER_LANES), axis=-1, keepdims=True)
    e1 = jnp.exp(v1 - v0)
    g0 = p_g / (1.0 + e1)
    g1 = p_g * e1 / (1.0 + e1)
    return i0 - MOE_GROUPS, i1 - MOE_GROUPS, g0, g1


def _norm_router_kernel(h_ref, g_ref, w_ref, b_ref, o_ref, eid_ref, gate_ref):
    y = _rms(h_ref[...], g_ref[...])
    o_ref[...] = y
    logits = jnp.dot(y, w_ref[...], preferred_element_type=F32, precision=lax.Precision.HIGHEST) + b_ref[...]
    e0, e1, g0, g1 = _route(logits)
    eid_ref[:, 0:1] = e0
    eid_ref[:, 1:2] = e1
    gate_ref[:, 0:1] = g0
    gate_ref[:, 1:2] = g1


def _row_grid(t, tm):
    return (pl.cdiv(t, tm),)


def _rmsnorm(h, g, layer, out_dtype=BF16, rows=None, row_block0=0, tm=ROW_TILE):
    t, d = h.shape
    rows = t if rows is None else rows
    tm = min(tm, rows)
    return pl.pallas_call(
        _norm_kernel,
        grid=_row_grid(rows, tm),
        in_specs=[pl.BlockSpec((tm, d), lambda i: (row_block0 + i, 0)),
                  pl.BlockSpec((None, 1, d), lambda i: (layer, 0, 0))],
        out_specs=pl.BlockSpec((tm, d), lambda i: (i, 0)),
        out_shape=jax.ShapeDtypeStruct((rows, d), out_dtype),
        compiler_params=_params("parallel"),
    )(h, g)


def _rmsnorm_proj(h, g, layer, w_small, tm=ROW_TILE):
    t, d = h.shape
    ns = w_small.shape[1]
    return pl.pallas_call(
        _norm_proj_kernel,
        grid=_row_grid(t, tm),
        in_specs=[pl.BlockSpec((tm, d), lambda i: (i, 0)),
                  pl.BlockSpec((None, 1, d), lambda i: (layer, 0, 0)),
                  pl.BlockSpec((d, ns), lambda i: (0, 0))],
        out_specs=[pl.BlockSpec((tm, d), lambda i: (i, 0)),
                   pl.BlockSpec((tm, ns), lambda i: (i, 0))],
        out_shape=[jax.ShapeDtypeStruct((t, d), BF16), jax.ShapeDtypeStruct((t, ns), F32)],
        compiler_params=_params("parallel"),
    )(h, g, w_small)


def _rmsnorm_router(h, g, layer, w_router, b_router, tm=ROW_TILE // 2):
    t, d = h.shape
    return pl.pallas_call(
        _norm_router_kernel,
        grid=_row_grid(t, tm),
        in_specs=[pl.BlockSpec((tm, d), lambda i: (i, 0)),
                  pl.BlockSpec((None, 1, d), lambda i: (layer, 0, 0)),
                  pl.BlockSpec((d, ROUTER_LANES), lambda i: (0, 0)),
                  pl.BlockSpec((1, ROUTER_LANES), lambda i: (0, 0))],
        out_specs=[pl.BlockSpec((tm, d), lambda i: (i, 0)),
                   pl.BlockSpec((tm, MOE_TOP_K), lambda i: (i, 0)),
                   pl.BlockSpec((tm, MOE_TOP_K), lambda i: (i, 0))],
        out_shape=[jax.ShapeDtypeStruct((t, d), F32),
                   jax.ShapeDtypeStruct((t, MOE_TOP_K), jnp.int32),
                   jax.ShapeDtypeStruct((t, MOE_TOP_K), F32)],
        compiler_params=_params("parallel"),
    )(h, g, w_router, b_router)


def _mm_kernel(*refs, mode, tail, split):
    a_ref, refs = refs[0], refs[1:]
    at_ref = a_ref
    if split:
        at_ref, refs = refs[0], refs[1:]
    if mode == "plain":
        w_ref, o_ref, wb_ref = refs
    elif mode == "res":
        w_ref, r_ref, o_ref, wb_ref = refs
    else:
        w_ref, r_ref, p_ref, wp_ref, o_ref, wb_ref, wpb_ref = refs
    m = pl.program_id(1)

    @pl.when(m == 0)
    def _():
        _cast_rows(w_ref, wb_ref)
        if mode == "ple":
            _cast_rows(wp_ref, wpb_ref)

    def tile(src_ref, rows):
        acc = jnp.dot(src_ref[rows, :], wb_ref[...], preferred_element_type=F32)
        if mode == "plain":
            o_ref[rows, :] = acc.astype(o_ref.dtype)
        elif mode == "res":
            o_ref[rows, :] = r_ref[rows, :] + acc
        else:
            pp = jnp.dot(p_ref[rows, :].astype(BF16), wpb_ref[...], preferred_element_type=F32)
            o_ref[rows, :] = r_ref[rows, :] + pp * _sigmoid(acc)

    if tail == 0:
        tile(a_ref, slice(None))
    else:
        last = pl.num_programs(1) - 1
        pl.when(m < last)(lambda: tile(a_ref, slice(None)))
        pl.when(m == last)(lambda: tile(at_ref, slice(0, tail)))


def _matmul(a, w, layer, n_cols, mode="plain", res=None, p=None, wp=None, a_tail=None, out_dtype=F32,
            tm=ROW_TILE, tn=COL_TILE):
    k = a.shape[1]
    split = a_tail is not None
    if split:
        assert a.shape[0] % tm == 0 and a_tail.shape[0] < tm
        t, tail = a.shape[0] + a_tail.shape[0], a_tail.shape[0]
    else:
        t, tail = a.shape[0], a.shape[0] % tm
    n_full = t // tm
    tn = min(tn, n_cols)
    assert n_cols % tn == 0
    grid = (n_cols // tn, pl.cdiv(t, tm))
    if split:
        in_specs = [pl.BlockSpec((tm, k), lambda n, m: (jnp.minimum(m, n_full - 1), 0)),
                    pl.BlockSpec((tail, k), lambda n, m: (0, 0))]
        args = [a, a_tail]
    else:
        in_specs = [pl.BlockSpec((tm, k), lambda n, m: (m, 0))]
        args = [a]
    in_specs.append(pl.BlockSpec((None, k, tn), lambda n, m: (layer, 0, n)))
    args.append(w)
    scratch = [pltpu.VMEM((k, tn), BF16)]
    if mode in ("res", "ple"):
        in_specs.append(pl.BlockSpec((tm, tn), lambda n, m: (m, n)))
        args.append(res)
    if mode == "ple":
        kp = p.shape[-1]
        in_specs += [pl.BlockSpec((None, tm, kp), lambda n, m: (layer, m, 0)),
                     pl.BlockSpec((None, kp, tn), lambda n, m: (layer, 0, n))]
        args += [p, wp]
        scratch.append(pltpu.VMEM((kp, tn), BF16))
    return pl.pallas_call(
        functools.partial(_mm_kernel, mode=mode, tail=tail, split=split),
        grid=grid,
        in_specs=in_specs,
        out_specs=pl.BlockSpec((tm, tn), lambda n, m: (m, n)),
        out_shape=jax.ShapeDtypeStruct((t, n_cols), out_dtype),
        scratch_shapes=scratch,
        compiler_params=_params("arbitrary", "arbitrary"),
    )(*args)


def _softmax_sink_pv(s, sk, vb):
    m = jnp.maximum(jnp.max(s, axis=-1, keepdims=True), sk)
    p = jnp.exp(s - m)
    den = jnp.sum(p, axis=-1, keepdims=True) + jnp.exp(sk - m)
    return jnp.dot((p / den).astype(BF16), vb, preferred_element_type=F32)


_NT = (((1,), (1,)), ((), ()))


def _swa_prompt_kernel(sinks_ref, q_ref, kp_ref, kc_ref, vp_ref, vc_ref, o_ref, *, group):
    kv = pl.program_id(1)
    n = pl.program_id(2)
    w, hd = kc_ref.shape
    kk = jnp.concatenate([kp_ref[...], kc_ref[...]], axis=0).astype(BF16)
    vv = jnp.concatenate([vp_ref[...], vc_ref[...]], axis=0).astype(BF16)
    qi = lax.broadcasted_iota(jnp.int32, (w, 2 * w), 0)
    kj = lax.broadcasted_iota(jnp.int32, (w, 2 * w), 1)
    in_win = (kj > qi) & (kj <= qi + w)
    lo = jnp.where(n > 0, 0, w)
    mask = in_win & (kj >= lo)
    scale = hd ** -0.5
    for g in range(group):
        qg = (q_ref[:, g * hd:(g + 1) * hd] * scale).astype(BF16)
        s = lax.dot_general(qg, kk, _NT, preferred_element_type=F32)
        s = jnp.where(mask, s, -jnp.inf)
        o = _softmax_sink_pv(s, sinks_ref[kv * group + g], vv)
        o_ref[:, g * hd:(g + 1) * hd] = o.astype(o_ref.dtype)


def _swa_prompt(qkv, sinks, batch, seq, kvh, t_out):
    hd, w = SWA_HEAD_DIM, WINDOW
    qd = qkv.shape[1] - 2 * kvh * hd
    group = qd // (kvh * hd)
    nb = seq // w
    kcol = qd // hd
    vcol = kcol + kvh

    def cur(b, k, n):
        return b * nb + n

    def prev(b, k, n):
        return b * nb + jnp.maximum(n - 1, 0)

    return pl.pallas_call(
        functools.partial(_swa_prompt_kernel, group=group),
        grid=(batch, kvh, nb),
        in_specs=[pl.BlockSpec(memory_space=pltpu.SMEM),
                  pl.BlockSpec((w, group * hd), lambda b, k, n: (cur(b, k, n), k)),
                  pl.BlockSpec((w, hd), lambda b, k, n: (prev(b, k, n), kcol + k)),
                  pl.BlockSpec((w, hd), lambda b, k, n: (cur(b, k, n), kcol + k)),
                  pl.BlockSpec((w, hd), lambda b, k, n: (prev(b, k, n), vcol + k)),
                  pl.BlockSpec((w, hd), lambda b, k, n: (cur(b, k, n), vcol + k))],
        out_specs=pl.BlockSpec((w, group * hd), lambda b, k, n: (cur(b, k, n), k)),
        out_shape=jax.ShapeDtypeStruct((t_out, qd), BF16),
        compiler_params=_params("parallel", "parallel", "arbitrary"),
    )(sinks, qkv, qkv, qkv, qkv, qkv)


def _swa_decode_kernel(sinks_ref, x_ref, kp_ref, vp_ref, nk_ref, nv_ref, o_ref, *, kvh, group):
    bb, w, kvd = kp_ref.shape
    hd = kvd // kvh
    qd = kvh * group * hd
    scale = hd ** -0.5
    row = lax.broadcasted_iota(jnp.int32, (w, kvd), 0)
    gi = lax.broadcasted_iota(jnp.int32, (group, 1), 0)
    for bi in range(bb):
        x = x_ref[bi:bi + 1, :]
        kc = jnp.where(row == w - 1, x[:, qd:qd + kvd], pltpu.roll(kp_ref[bi], w - 1, axis=0))
        vc = jnp.where(row == w - 1, x[:, qd + kvd:], pltpu.roll(vp_ref[bi], w - 1, axis=0))
        nk_ref[bi] = kc
        nv_ref[bi] = vc
        kcb = kc.astype(BF16)
        vcb = vc.astype(BF16)
        pieces = []
        for kv in range(kvh):
            c0 = kv * group * hd
            qg = jnp.concatenate([x[:, c0 + g * hd:c0 + (g + 1) * hd] for g in range(group)], axis=0)
            qg = (qg * scale).astype(BF16)
            s = lax.dot_general(qg, kcb[:, kv * hd:(kv + 1) * hd], _NT, preferred_element_type=F32)
            sk = jnp.zeros((group, 1), F32)
            for g in range(group):
                sk = jnp.where(gi == g, sinks_ref[kv * group + g], sk)
            og = _softmax_sink_pv(s, sk, vcb[:, kv * hd:(kv + 1) * hd])
            pieces += [og[g:g + 1, :] for g in range(group)]
        o_ref[bi:bi + 1, :] = jnp.concatenate(pieces, axis=1)


def _swa_decode(qkv, sinks, k_past, v_past, row0, bb=8):
    b, w, kvh, hd = k_past.shape
    kvd = kvh * hd
    qd = qkv.shape[1] - 2 * kvd
    group = qd // kvd
    bb = min(bb, b)
    blk0 = row0 // bb
    cache_spec = pl.BlockSpec((bb, w, kvd), lambda i: (i, 0, 0))
    nk, nv, o = pl.pallas_call(
        functools.partial(_swa_decode_kernel, kvh=kvh, group=group),
        grid=(b // bb,),
        in_specs=[pl.BlockSpec(memory_space=pltpu.SMEM),
                  pl.BlockSpec((bb, qkv.shape[1]), lambda i: (blk0 + i, 0)),
                  cache_spec, cache_spec],
        out_specs=[cache_spec, cache_spec, pl.BlockSpec((bb, qd), lambda i: (i, 0))],
        out_shape=[jax.ShapeDtypeStruct((b, w, kvd), F32), jax.ShapeDtypeStruct((b, w, kvd), F32),
                   jax.ShapeDtypeStruct((b, qd), F32)],
        compiler_params=_params("parallel"),
    )(sinks, qkv, k_past.reshape(b, w, kvd), v_past.reshape(b, w, kvd))
    return o, nk.reshape(b, w, kvh, hd), nv.reshape(b, w, kvh, hd)


def _log_decay(gl, wgu, bg):
    x = jnp.dot(gl.astype(BF16), wgu.astype(BF16), preferred_element_type=F32) + bg
    return (jnp.minimum(x, 0.0) - jnp.log(1.0 + jnp.exp(-jnp.abs(x)))) * (1.0 / GLA_TAU)


def _gla_out(o, ng, r):
    return _rms(o, ng) * (r * _sigmoid(r))


def _gla_prompt_kernel(q_ref, k_ref, v_ref, r_ref, gl_ref, wgu_ref, bg_ref, ng_ref, o_ref, s_ref,
                       cum_ref, qs_ref, att_ref):
    c, dkh = q_ref.shape

    @pl.when(pl.program_id(2) == 0)
    def _():
        s_ref[...] = jnp.zeros_like(s_ref)

    g = _log_decay(gl_ref[...], wgu_ref[...], bg_ref[...])
    row = lax.broadcasted_iota(jnp.int32, (c, dkh), 0)
    cum = g
    sh = 1
    while sh < c:
        cum = cum + jnp.where(row >= sh, pltpu.roll(cum, sh, axis=0), 0.0)
        sh *= 2
    q = q_ref[...] * (dkh ** -0.5)
    k = k_ref[...]
    vb = v_ref[...].astype(BF16)
    cum_ref[...] = cum
    qs_ref[...] = q

    nd = GLA_DIAG

    def diag_block(b, carry):
        r0 = pl.multiple_of(b * nd, nd)
        qb = qs_ref[pl.ds(r0, nd), :]
        kb = k_ref[pl.ds(r0, nd), :]
        cb = cum_ref[pl.ds(r0, nd), :]
        lane = lax.broadcasted_iota(jnp.int32, (nd, c), 1)
        sub = lax.broadcasted_iota(jnp.int32, (nd, 1), 0)
        acc = jnp.zeros((nd, c), F32)
        for j in range(nd):
            u = qb * kb[j:j + 1, :] * jnp.exp(jnp.minimum(cb - cb[j:j + 1, :], 0.0))
            col = jnp.where(sub >= j, jnp.sum(u, axis=-1, keepdims=True), 0.0)
            acc = jnp.where(lane == r0 + j, col, acc)
        att_ref[pl.ds(r0, nd), :] = acc
        return carry

    lax.fori_loop(0, c // nd, diag_block, 0)

    att = att_ref[...]
    ri = lax.broadcasted_iota(jnp.int32, (c, c), 0)
    ci = lax.broadcasted_iota(jnp.int32, (c, c), 1)
    s = c // 2
    while s >= nd:
        nblk = c // (2 * s)
        cref = cum.reshape(nblk, 2 * s, dkh)[:, s - 1:s, :]
        cref = jnp.broadcast_to(cref, (nblk, 2 * s, dkh)).reshape(c, dkh)
        later = (row & s) != 0
        e = jnp.exp(jnp.minimum(jnp.where(later, cum - cref, cref - cum), 0.0))
        a_s = lax.dot_general((q * e).astype(BF16), (k * e).astype(BF16), _NT, preferred_element_type=F32)
        shift = s.bit_length() - 1
        pair = (((ri ^ ci) >> shift) == 1) & ((ri & s) != 0)
        att = att + jnp.where(pair, a_s, 0.0)
        s //= 2

    st = s_ref[...]
    o = jnp.dot((q * jnp.exp(cum)).astype(BF16), st.astype(BF16), preferred_element_type=F32)
    o = o + jnp.dot(att.astype(BF16), vb, preferred_element_type=F32)
    o_ref[...] = _gla_out(o, ng_ref[...], r_ref[...]).astype(o_ref.dtype)

    last = cum[c - 1:c, :]
    kd_t = (k * jnp.exp(last - cum)).T.astype(BF16)
    e_col = jnp.exp(cum.T[:, c - 1:c])
    s_ref[...] = e_col * st + jnp.dot(kd_t, vb, preferred_element_type=F32)


def _gla_prompt(proj, gl, w_gate_up, b_gate, norm_g, batch, seq, t_out):
    h = GLA_HEADS
    dk = w_gate_up.shape[-1]
    dkh = dk // h
    dv = (proj.shape[1] - 2 * dk) // 2
    dvh = dv // h
    c = GLA_CHUNK
    nc = seq // c

    def rows(b, hh, i):
        return b * nc + i

    return pl.pallas_call(
        _gla_prompt_kernel,
        grid=(batch, h, nc),
        in_specs=[pl.BlockSpec((c, dkh), lambda b, hh, i: (rows(b, hh, i), hh)),
                  pl.BlockSpec((c, dkh), lambda b, hh, i: (rows(b, hh, i), h + hh)),
                  pl.BlockSpec((c, dvh), lambda b, hh, i: (rows(b, hh, i), 2 * dk // dvh + hh)),
                  pl.BlockSpec((c, dvh), lambda b, hh, i: (rows(b, hh, i), (2 * dk + dv) // dvh + hh)),
                  pl.BlockSpec((c, GLA_GATE_RANK), lambda b, hh, i: (rows(b, hh, i), 0)),
                  pl.BlockSpec((None, GLA_GATE_RANK, dkh), lambda b, hh, i: (0, 0, hh)),
                  pl.BlockSpec((1, dkh), lambda b, hh, i: (0, hh)),
                  pl.BlockSpec((1, dvh), lambda b, hh, i: (0, 0))],
        out_specs=[pl.BlockSpec((c, dvh), lambda b, hh, i: (rows(b, hh, i), hh)),
                   pl.BlockSpec((None, None, dkh, dvh), lambda b, hh, i: (b, hh, 0, 0))],
        out_shape=[jax.ShapeDtypeStruct((t_out, dv), BF16),
                   jax.ShapeDtypeStruct((batch, h, dkh, dvh), F32)],
        scratch_shapes=[pltpu.VMEM((c, dkh), F32), pltpu.VMEM((c, dkh), F32), pltpu.VMEM((c, c), F32)],
        compiler_params=_params("parallel", "parallel", "arbitrary"),
    )(proj, proj, proj, proj, gl, w_gate_up, b_gate, norm_g)


def _gla_gate_kernel(gl_ref, wgu_ref, bg_ref, o_ref):
    o_ref[...] = _log_decay(gl_ref[...], wgu_ref[...], bg_ref[...])


def _gla_gate(gl, w_gate_up, b_gate, row0, rows):
    dk = w_gate_up.shape[-1]
    return pl.pallas_call(
        _gla_gate_kernel,
        grid=(1,),
        in_specs=[pl.BlockSpec((rows, GLA_GATE_RANK), lambda i: (row0 // rows, 0)),
                  pl.BlockSpec((None, GLA_GATE_RANK, dk), lambda i: (0, 0, 0)),
                  pl.BlockSpec((1, dk), lambda i: (0, 0))],
        out_specs=pl.BlockSpec((rows, dk), lambda i: (0, 0)),
        out_shape=jax.ShapeDtypeStruct((rows, dk), F32),
        compiler_params=_params("arbitrary"),
    )(gl, w_gate_up, b_gate)


def _gla_decode_kernel(q_ref, v_ref, r_ref, kc_ref, gc_ref, ng_ref, s_ref, o_ref, ns_ref):
    dkh = q_ref.shape[-1]
    sn = jnp.exp(gc_ref[...]) * s_ref[...] + kc_ref[...] * v_ref[...]
    ns_ref[...] = sn
    q = jnp.broadcast_to(q_ref[...] * (dkh ** -0.5), (8, dkh)).astype(BF16)
    o = jnp.dot(q, sn.astype(BF16), preferred_element_type=F32)[0:1, :]
    o_ref[...] = _gla_out(o, ng_ref[...], r_ref[...])


def _gla_decode(proj_s, log_a, norm_g, state):
    b, h, dkh, dvh = state.shape
    dk, dv = h * dkh, h * dvh
    k_col = proj_s[:, 0, dk:2 * dk].reshape(b, h, dkh, 1)
    g_col = log_a.reshape(b, h, dkh, 1)
    col_spec = pl.BlockSpec((None, None, dkh, 1), lambda i, hh: (i, hh, 0, 0))
    st_spec = pl.BlockSpec((None, None, dkh, dvh), lambda i, hh: (i, hh, 0, 0))
    return pl.pallas_call(
        _gla_decode_kernel,
        grid=(b, h),
        in_specs=[pl.BlockSpec((None, 1, dkh), lambda i, hh: (i, 0, hh)),
                  pl.BlockSpec((None, 1, dvh), lambda i, hh: (i, 0, 2 * dk // dvh + hh)),
                  pl.BlockSpec((None, 1, dvh), lambda i, hh: (i, 0, (2 * dk + dv) // dvh + hh)),
                  col_spec, col_spec,
                  pl.BlockSpec((1, dvh), lambda i, hh: (0, 0)),
                  st_spec],
        out_specs=[pl.BlockSpec((None, 1, dvh), lambda i, hh: (i, 0, hh)), st_spec],
        out_shape=[jax.ShapeDtypeStruct((b, 1, dv), F32), jax.ShapeDtypeStruct(state.shape, F32)],
        compiler_params=_params("parallel", "parallel"),
    )(proj_s, proj_s, proj_s, k_col, g_col, norm_g, state)


def _moe_kernel(ie_ref, ir_ref, in_ref, tok_ref, dst_ref, x_hbm, wg_ref, wu_ref, wd_ref, y_hbm,
                xbuf, ybuf, wgb, wub, wdb, gsem, ssem):
    i = pl.program_id(0)
    f = pl.program_id(1)
    nf = pl.num_programs(1)
    n = in_ref[i]
    row0 = ir_ref[i]
    rb = xbuf.shape[0]
    sub = MOE_SUB_BLOCK if rb % MOE_SUB_BLOCK == 0 else rb

    def gather(r):
        return pltpu.make_async_copy(x_hbm.at[pl.ds(tok_ref[row0 + r], 1)], xbuf.at[pl.ds(r, 1)], gsem)

    def scatter(r):
        return pltpu.make_async_copy(ybuf.at[pl.ds(r, 1)], y_hbm.at[pl.ds(dst_ref[row0 + r], 1)], ssem)

    def for_rows(fn):
        def body(r, c):
            fn(r)
            return c
        lax.fori_loop(0, n, body, 0)

    @pl.when((i == 0) & (f == 0))
    def _():
        xbuf[...] = jnp.zeros_like(xbuf)

    @pl.when(f == 0)
    def _():
        for_rows(lambda r: gather(r).start())
        for_rows(lambda r: gather(r).wait())

    @pl.when(n > 0)
    def _():
        _cast_rows(wg_ref, wgb)
        _cast_rows(wu_ref, wub)
        _cast_rows(wd_ref, wdb)

    for sb in range(rb // sub):
        sl = slice(sb * sub, (sb + 1) * sub)

        @pl.when(sb * sub < n)
        def _():
            xs = xbuf[sl, :].astype(BF16)
            hg = jnp.dot(xs, wgb[...], preferred_element_type=F32)
            hu = jnp.dot(xs, wub[...], preferred_element_type=F32)
            act = (hg * _sigmoid(hg) * hu).astype(BF16)
            y = jnp.dot(act, wdb[...], preferred_element_type=F32)

            @pl.when(f == 0)
            def _():
                ybuf[sl, :] = y

            @pl.when(f > 0)
            def _():
                ybuf[sl, :] += y

    @pl.when(f == nf - 1)
    def _():
        for_rows(lambda r: scatter(r).start())
        for_rows(lambda r: scatter(r).wait())


def _moe_experts(xn, eid, w_gate, w_up, w_down, layer):
    t, d = xn.shape
    n_exp, _, dff = w_gate.shape[1:]
    a = t * MOE_TOP_K
    rb = min(MOE_ROW_BLOCK, -(-a // 8) * 8)
    fc = min(MOE_FF_TILE, dff)
    nf = dff // fc
    n_items = n_exp + a // rb

    flat = eid.reshape(-1)
    order = jnp.argsort(flat, stable=True).astype(jnp.int32)
    counts = jnp.sum((flat[:, None] == jnp.arange(n_exp, dtype=jnp.int32)[None, :]).astype(jnp.int32), axis=0)
    starts = jnp.cumsum(counts) - counts
    per_e = (counts + rb - 1) // rb
    item_end = jnp.cumsum(per_e)
    item_start = item_end - per_e
    ii = jnp.arange(n_items, dtype=jnp.int32)
    valid = ii < item_end[-1]
    e_of = jnp.minimum(jnp.searchsorted(item_end, ii, side="right").astype(jnp.int32), n_exp - 1)
    local = ii - item_start[e_of]
    item_rows = jnp.where(valid, jnp.clip(counts[e_of] - local * rb, 0, rb), 0).astype(jnp.int32)
    item_row0 = jnp.where(valid, starts[e_of] + local * rb, 0).astype(jnp.int32)
    e_last = jnp.max(jnp.where(valid, e_of, 0))
    item_e = jnp.where(valid, e_of, e_last).astype(jnp.int32)
    tok = order // MOE_TOP_K

    def w_in_map(i, f, ie, ir, inn, tk, ds_):
        return (layer, ie[i], 0, jnp.where(inn[i] > 0, f, nf - 1))

    def w_out_map(i, f, ie, ir, inn, tk, ds_):
        return (layer, ie[i], jnp.where(inn[i] > 0, f, nf - 1), 0)

    grid_spec = pltpu.PrefetchScalarGridSpec(
        num_scalar_prefetch=5,
        grid=(n_items, nf),
        in_specs=[pl.BlockSpec(memory_space=pl.ANY),
                  pl.BlockSpec((None, None, d, fc), w_in_map),
                  pl.BlockSpec((None, None, d, fc), w_in_map),
                  pl.BlockSpec((None, None, fc, d), w_out_map)],
        out_specs=pl.BlockSpec(memory_space=pl.ANY),
        scratch_shapes=[pltpu.VMEM((rb, d), F32), pltpu.VMEM((rb, d), F32),
                        pltpu.VMEM((d, fc), BF16), pltpu.VMEM((d, fc), BF16), pltpu.VMEM((fc, d), BF16),
                        pltpu.SemaphoreType.DMA(()), pltpu.SemaphoreType.DMA(())],
    )
    return pl.pallas_call(
        _moe_kernel,
        grid_spec=grid_spec,
        out_shape=jax.ShapeDtypeStruct((a, d), F32),
        compiler_params=_params("arbitrary", "arbitrary"),
    )(item_e, item_row0, item_rows, tok, order, xn, w_gate, w_up, w_down)


def _combine_norm_kernel(h_ref, y_ref, gate_ref, g_ref, ho_ref, a_ref):
    d = h_ref.shape[1]
    h = h_ref[...] + (y_ref[:, :d] * gate_ref[:, 0:1] + y_ref[:, d:] * gate_ref[:, 1:2])
    ho_ref[...] = h
    a_ref[...] = _rms(h, g_ref[...]).astype(a_ref.dtype)


def _combine_norm(h, y2, gates, g, layer, tm=ROW_TILE // 2):
    t, d = h.shape
    return pl.pallas_call(
        _combine_norm_kernel,
        grid=_row_grid(t, tm),
        in_specs=[pl.BlockSpec((tm, d), lambda i: (i, 0)),
                  pl.BlockSpec((tm, MOE_TOP_K * d), lambda i: (i, 0)),
                  pl.BlockSpec((tm, MOE_TOP_K), lambda i: (i, 0)),
                  pl.BlockSpec((None, 1, d), lambda i: (layer, 0, 0))],
        out_specs=[pl.BlockSpec((tm, d), lambda i: (i, 0)), pl.BlockSpec((tm, d), lambda i: (i, 0))],
        out_shape=[jax.ShapeDtypeStruct((t, d), F32), jax.ShapeDtypeStruct((t, d), BF16)],
        compiler_params=_params("parallel"),
    )(h, y2.reshape(t, MOE_TOP_K * d), gates, g)


def kernel(x_prompt, x_sample, cache_swa_k, cache_swa_v, state_gla, p_prompt, p_sample, ln_mix, ln_ffn, ln_ple, ln_final, swa_w_in, swa_sinks, swa_w_out, gla_w_in, gla_w_gate_up, gla_b_gate, gla_norm, gla_w_out, moe_w_group, moe_b_group, moe_w_expert, moe_b_expert, moe_w_gate, moe_w_up, moe_w_down, ple_w_proj, ple_w_gate):
    batch, seq, d = x_prompt.shape
    dec = x_sample.shape[0]
    assert x_sample.shape[1] == 1
    tp = batch * seq
    t = tp + dec
    depth = ln_mix.shape[0]
    kvh = cache_swa_k.shape[3]
    kvd = kvh * SWA_HEAD_DIM
    dk = gla_w_gate_up.shape[-1]
    dv = (gla_w_in.shape[-1] - GLA_GATE_RANK - 2 * dk) // 2

    h = jnp.concatenate([x_prompt.reshape(tp, d), x_sample.reshape(dec, d)], axis=0)
    p_all = jnp.concatenate([p_prompt.reshape(depth, tp, -1), p_sample.reshape(depth, dec, -1)], axis=1)
    pad = ROUTER_LANES - MOE_GROUPS - MOE_EXPERTS
    w_router = jnp.concatenate([moe_w_group, moe_w_expert, jnp.zeros((depth, d, pad), F32)], axis=-1)
    b_router = jnp.concatenate([moe_b_group, moe_b_expert, jnp.zeros((depth, pad), F32)], axis=-1)
    ln_mix, ln_ffn, ln_ple = (g.reshape(depth, 1, d) for g in (ln_mix, ln_ffn, ln_ple))

    new_k_p, new_v_p, new_s_p, new_k_s, new_v_s, new_s_s = [], [], [], [], [], []
    for i in range(depth):
        j = i // 2
        if i % 2 == 0:
            a = _rmsnorm(h, ln_mix, i)
            qkv = _matmul(a, swa_w_in, j, swa_w_in.shape[-1])
            qd = qkv.shape[1] - 2 * kvd
            o = _swa_prompt(qkv, swa_sinks[j], batch, seq, kvh, tp)
            o_s, nk, nv = _swa_decode(qkv, swa_sinks[j], cache_swa_k[j], cache_swa_v[j], tp)
            kv_p = qkv[:tp].reshape(batch, seq, -1)[:, seq - WINDOW:, qd:]
            new_k_p.append(kv_p[..., :kvd].reshape(batch, WINDOW, kvh, SWA_HEAD_DIM))
            new_v_p.append(kv_p[..., kvd:].reshape(batch, WINDOW, kvh, SWA_HEAD_DIM))
            new_k_s.append(nk)
            new_v_s.append(nv)
            h = _matmul(o, swa_w_out, j, d, mode="res", res=h, a_tail=o_s.astype(BF16))
        else:
            w_gl = gla_w_in[j][:, 2 * dk + 2 * dv:]
            a, gl = _rmsnorm_proj(h, ln_mix, i, w_gl)
            proj = _matmul(a, gla_w_in, j, 2 * dk + 2 * dv)
            o, s_p = _gla_prompt(proj, gl, gla_w_gate_up[j:j + 1], gla_b_gate[j:j + 1], gla_norm[j:j + 1], batch, seq, tp)
            log_a = _gla_gate(gl, gla_w_gate_up[j:j + 1], gla_b_gate[j:j + 1], tp, dec)
            o_s, s_s = _gla_decode(proj[tp:].reshape(dec, 1, -1), log_a, gla_norm[j:j + 1], state_gla[j])
            new_s_p.append(s_p)
            new_s_s.append(s_s)
            h = _matmul(o, gla_w_out, j, d, mode="res", res=h, a_tail=o_s.reshape(dec, dv).astype(BF16))
        xn, eid, gates = _rmsnorm_router(h, ln_ffn, i, w_router[i], b_router[i:i + 1])
        y2 = _moe_experts(xn, eid, moe_w_gate, moe_w_up, moe_w_down, i)
        h, a3 = _combine_norm(h, y2, gates, ln_ple, i)
        h = _matmul(a3, ple_w_gate, i, d, mode="ple", res=h, p=p_all, wp=ple_w_proj)

    g_fin = ln_final.reshape(1, 1, d)
    y_p = _rmsnorm(h, g_fin, 0, out_dtype=F32, rows=tp)
    y_s = _rmsnorm(h, g_fin, 0, out_dtype=F32, rows=dec, row_block0=tp // dec, tm=dec)
    return (y_p.reshape(batch, seq, d), y_s.reshape(dec, 1, d),
            jnp.stack(new_k_p), jnp.stack(new_v_p), jnp.stack(new_s_p),
            jnp.stack(new_k_s), jnp.stack(new_v_s), jnp.stack(new_s_s))
```

```python
import functools

import jax
import jax.numpy as jnp
from jax import lax
from jax.experimental import pallas as pl
from jax.experimental.pallas import tpu as pltpu

F32 = jnp.float32
BF16 = jnp.bfloat16

EPS = 1e-6
WINDOW = 128
SWA_HEAD_DIM = 128
GLA_HEADS = 4
GLA_GATE_RANK = 16
GLA_TAU = 16.0
GLA_CHUNK = 128
MOE_GROUPS = 8
MOE_EXPERTS_PER_GROUP = 8
MOE_EXPERTS = MOE_GROUPS * MOE_EXPERTS_PER_GROUP
MOE_TOP_K = 2
ROUTER_LANES = 128

V7X_VMEM_LIMIT_BYTES = 60 * 1024 * 1024
ROW_TILE = 512
MM_ROW_TILE = 1024
COL_TILE = 512
MOE_ROW_BLOCK = 512
MOE_SUB_BLOCK = 256
MOE_FF_TILE = 256
CAST_ROWS = 256


def _params(*sem):
    return pltpu.CompilerParams(dimension_semantics=sem, vmem_limit_bytes=V7X_VMEM_LIMIT_BYTES)


def _sigmoid(x):
    return 1.0 / (1.0 + jnp.exp(-x))


def _cast_rows(src_ref, dst_ref):
    rows = src_ref.shape[0]
    step = min(CAST_ROWS, rows)

    def body(i, c):
        r = pl.multiple_of(i * step, step)
        dst_ref[pl.ds(r, step), :] = src_ref[pl.ds(r, step), :].astype(BF16)
        return c

    lax.fori_loop(0, rows // step, body, 0)


def _rms(x, g):
    ms = jnp.mean(x * x, axis=-1, keepdims=True)
    return x * lax.rsqrt(ms + EPS) * g


def _norm_kernel(h_ref, g_ref, o_ref):
    o_ref[...] = _rms(h_ref[...], g_ref[...]).astype(o_ref.dtype)


def _norm_proj_kernel(h_ref, g_ref, w_ref, o_ref, s_ref):
    y = _rms(h_ref[...], g_ref[...]).astype(BF16)
    o_ref[...] = y
    s_ref[...] = jnp.dot(y, w_ref[...].astype(BF16), preferred_element_type=F32)


def _route(logits):
    lane = lax.broadcasted_iota(jnp.int32, logits.shape, 1)
    neg = jnp.float32(-jnp.inf)
    is_g = lane < MOE_GROUPS
    lg = jnp.where(is_g, logits, neg)
    mg = jnp.max(lg, axis=-1, keepdims=True)
    g_sel = jnp.min(jnp.where(lg == mg, lane, ROUTER_LANES), axis=-1, keepdims=True)
    p_g = 1.0 / jnp.sum(jnp.where(is_g, jnp.exp(lg - mg), 0.0), axis=-1, keepdims=True)
    lo = MOE_GROUPS + g_sel * MOE_EXPERTS_PER_GROUP
    in_grp = (lane >= lo) & (lane < lo + MOE_EXPERTS_PER_GROUP)
    le = jnp.where(in_grp, logits, neg)
    v0 = jnp.max(le, axis=-1, keepdims=True)
    i0 = jnp.min(jnp.where(le == v0, lane, ROUTER_LANES), axis=-1, keepdims=True)
    le1 = jnp.where(lane == i0, neg, le)
    v1 = jnp.max(le1, axis=-1, keepdims=True)
    i1 = jnp.min(jnp.where(le1 == v1, lane, ROUTER_LANES), axis=-1, keepdims=True)
    e1 = jnp.exp(v1 - v0)
    g0 = p_g / (1.0 + e1)
    g1 = p_g * e1 / (1.0 + e1)
    return i0 - MOE_GROUPS, i1 - MOE_GROUPS, g0, g1


def _norm_router_kernel(h_ref, g_ref, w_ref, b_ref, o_ref, eid_ref, gate_ref):
    y = _rms(h_ref[...], g_ref[...])
    o_ref[...] = _pack_bf16_halves(y)
    logits = jnp.dot(y.astype(BF16), w_ref[...].astype(BF16), preferred_element_type=F32) + b_ref[...]
    e0, e1, g0, g1 = _route(logits)
    eid_ref[:, 0:1] = e0
    eid_ref[:, 1:2] = e1
    gate_ref[:, 0:1] = g0
    gate_ref[:, 1:2] = g1


def _row_grid(t, tm):
    return (pl.cdiv(t, tm),)


def _rmsnorm(h, g, layer, out_dtype=BF16, rows=None, row_block0=0, tm=ROW_TILE):
    t, d = h.shape
    rows = t if rows is None else rows
    tm = min(tm, rows)
    return pl.pallas_call(
        _norm_kernel,
        grid=_row_grid(rows, tm),
        in_specs=[pl.BlockSpec((tm, d), lambda i: (row_block0 + i, 0)),
                  pl.BlockSpec((None, 1, d), lambda i: (layer, 0, 0))],
        out_specs=pl.BlockSpec((tm, d), lambda i: (i, 0)),
        out_shape=jax.ShapeDtypeStruct((rows, d), out_dtype),
        compiler_params=_params("parallel"),
    )(h, g)


def _rmsnorm_proj(h, g, layer, w_small, tm=ROW_TILE):
    t, d = h.shape
    ns = w_small.shape[1]
    return pl.pallas_call(
        _norm_proj_kernel,
        grid=_row_grid(t, tm),
        in_specs=[pl.BlockSpec((tm, d), lambda i: (i, 0)),
                  pl.BlockSpec((None, 1, d), lambda i: (layer, 0, 0)),
                  pl.BlockSpec((d, ns), lambda i: (0, 0))],
        out_specs=[pl.BlockSpec((tm, d), lambda i: (i, 0)),
                   pl.BlockSpec((tm, ns), lambda i: (i, 0))],
        out_shape=[jax.ShapeDtypeStruct((t, d), BF16), jax.ShapeDtypeStruct((t, ns), F32)],
        compiler_params=_params("parallel"),
    )(h, g, w_small)


def _rmsnorm_router(h, g, layer, w_router, b_router, tm=ROW_TILE // 2):
    t, d = h.shape
    return pl.pallas_call(
        _norm_router_kernel,
        grid=_row_grid(t, tm),
        in_specs=[pl.BlockSpec((tm, d), lambda i: (i, 0)),
                  pl.BlockSpec((None, 1, d), lambda i: (layer, 0, 0)),
                  pl.BlockSpec((d, ROUTER_LANES), lambda i: (0, 0)),
                  pl.BlockSpec((1, ROUTER_LANES), lambda i: (0, 0))],
        out_specs=[pl.BlockSpec((tm, d // 2), lambda i: (i, 0)),
                   pl.BlockSpec((tm, MOE_TOP_K), lambda i: (i, 0)),
                   pl.BlockSpec((tm, MOE_TOP_K), lambda i: (i, 0))],
        out_shape=[jax.ShapeDtypeStruct((t, d // 2), jnp.uint32),
                   jax.ShapeDtypeStruct((t, MOE_TOP_K), jnp.int32),
                   jax.ShapeDtypeStruct((t, MOE_TOP_K), F32)],
        compiler_params=_params("parallel"),
    )(h, g, w_router, b_router)


def _mm_kernel(*refs, mode, tail, split_a, split_r):
    a_ref, refs = refs[0], refs[1:]
    at_ref = a_ref
    if split_a:
        at_ref, refs = refs[0], refs[1:]
    w_ref, refs = refs[0], refs[1:]
    r_ref = rt_ref = None
    if mode != "plain":
        r_ref, refs = refs[0], refs[1:]
        rt_ref = r_ref
        if split_r:
            rt_ref, refs = refs[0], refs[1:]
    if mode == "ple":
        p_ref, wp_ref, o_ref, wb_ref, wpb_ref = refs
    else:
        o_ref, wb_ref = refs
    m = pl.program_id(1)

    @pl.when(m == 0)
    def _():
        _cast_rows(w_ref, wb_ref)
        if mode == "ple":
            _cast_rows(wp_ref, wpb_ref)

    def tile(src_ref, res_ref, rows):
        acc = jnp.dot(src_ref[rows, :], wb_ref[...], preferred_element_type=F32)
        if mode == "plain":
            o_ref[rows, :] = acc.astype(o_ref.dtype)
        elif mode == "res":
            o_ref[rows, :] = res_ref[rows, :] + acc
        else:
            pp = jnp.dot(p_ref[rows, :].astype(BF16), wpb_ref[...], preferred_element_type=F32)
            o_ref[rows, :] = res_ref[rows, :] + pp * _sigmoid(acc)

    if tail == 0:
        tile(a_ref, r_ref, slice(None))
    else:
        last = pl.num_programs(1) - 1
        pl.when(m < last)(lambda: tile(a_ref, r_ref, slice(None)))
        pl.when(m == last)(lambda: tile(at_ref, rt_ref, slice(0, tail)))


def _matmul(a, w, layer, n_cols, mode="plain", res=None, p=None, wp=None, a_tail=None, res_tail=None,
            out_dtype=F32, tm=MM_ROW_TILE, tn=COL_TILE):
    k = a.shape[1]
    tm = min(tm, a.shape[0] // 256 * 256)
    split_a = a_tail is not None
    split_r = res_tail is not None
    if split_a:
        assert a.shape[0] % tm == 0 and a_tail.shape[0] < tm
        t, tail = a.shape[0] + a_tail.shape[0], a_tail.shape[0]
    else:
        t, tail = a.shape[0], a.shape[0] % tm
    if split_r:
        assert res.shape[0] == t - tail and res_tail.shape[0] == tail
    n_full = t // tm

    def full_rows(n, m):
        return jnp.minimum(m, n_full - 1)

    tn = min(tn, n_cols)
    assert n_cols % tn == 0
    grid = (n_cols // tn, pl.cdiv(t, tm))
    if split_a:
        in_specs = [pl.BlockSpec((tm, k), lambda n, m: (full_rows(n, m), 0)),
                    pl.BlockSpec((tail, k), lambda n, m: (0, 0))]
        args = [a, a_tail]
    else:
        in_specs = [pl.BlockSpec((tm, k), lambda n, m: (m, 0))]
        args = [a]
    in_specs.append(pl.BlockSpec((None, k, tn), lambda n, m: (layer, 0, n)))
    args.append(w)
    scratch = [pltpu.VMEM((k, tn), BF16)]
    if mode in ("res", "ple"):
        if split_r:
            in_specs += [pl.BlockSpec((tm, tn), lambda n, m: (full_rows(n, m), n)),
                         pl.BlockSpec((tail, tn), lambda n, m: (0, n))]
            args += [res, res_tail]
        else:
            in_specs.append(pl.BlockSpec((tm, tn), lambda n, m: (m, n)))
            args.append(res)
    if mode == "ple":
        kp = p.shape[-1]
        in_specs += [pl.BlockSpec((None, tm, kp), lambda n, m: (layer, m, 0)),
                     pl.BlockSpec((None, kp, tn), lambda n, m: (layer, 0, n))]
        args += [p, wp]
        scratch.append(pltpu.VMEM((kp, tn), BF16))
    return pl.pallas_call(
        functools.partial(_mm_kernel, mode=mode, tail=tail, split_a=split_a, split_r=split_r),
        grid=grid,
        in_specs=in_specs,
        out_specs=pl.BlockSpec((tm, tn), lambda n, m: (m, n)),
        out_shape=jax.ShapeDtypeStruct((t, n_cols), out_dtype),
        scratch_shapes=scratch,
        compiler_params=_params("arbitrary", "arbitrary"),
    )(*args)


def _softmax_sink_pv(s, sk, vb):
    m = jnp.maximum(jnp.max(s, axis=-1, keepdims=True), sk)
    p = jnp.exp(s - m)
    den = jnp.sum(p, axis=-1, keepdims=True) + jnp.exp(sk - m)
    return jnp.dot((p / den).astype(BF16), vb, preferred_element_type=F32)


_NT = (((1,), (1,)), ((), ()))


def _swa_prompt_kernel(sinks_ref, q_ref, kp_ref, kc_ref, vp_ref, vc_ref, o_ref, nk_ref, nv_ref, *, kvh, group):
    n = pl.program_id(1)
    w, kvd = kc_ref.shape
    hd = kvd // kvh
    kk = jnp.concatenate([kp_ref[...], kc_ref[...]], axis=0).astype(BF16)
    vv = jnp.concatenate([vp_ref[...], vc_ref[...]], axis=0).astype(BF16)
    qi = lax.broadcasted_iota(jnp.int32, (w, 2 * w), 0)
    kj = lax.broadcasted_iota(jnp.int32, (w, 2 * w), 1)
    mask = (kj > qi) & (kj <= qi + w) & (kj >= jnp.where(n > 0, 0, w))
    scale = hd ** -0.5
    for kv in range(kvh):
        kh = kk[:, kv * hd:(kv + 1) * hd]
        vh = vv[:, kv * hd:(kv + 1) * hd]
        for g in range(group):
            c0 = (kv * group + g) * hd
            qg = (q_ref[:, c0:c0 + hd] * scale).astype(BF16)
            s = lax.dot_general(qg, kh, _NT, preferred_element_type=F32)
            s = jnp.where(mask, s, -jnp.inf)
            o = _softmax_sink_pv(s, sinks_ref[kv * group + g], vh)
            o_ref[:, c0:c0 + hd] = o.astype(o_ref.dtype)

    @pl.when(n == pl.num_programs(1) - 1)
    def _():
        nk_ref[...] = kc_ref[...]
        nv_ref[...] = vc_ref[...]


def _swa_prompt(qkv, sinks, batch, seq, kvh):
    hd, w = SWA_HEAD_DIM, WINDOW
    kvd = kvh * hd
    qd = qkv.shape[1] - 2 * kvd
    group = qd // kvd
    nb = seq // w
    kcol = qd // kvd

    def cur(b, n):
        return b * nb + n

    def prev(b, n):
        return b * nb + jnp.maximum(n - 1, 0)

    last_spec = pl.BlockSpec((None, w, kvd), lambda b, n: (b, 0, 0))
    return pl.pallas_call(
        functools.partial(_swa_prompt_kernel, kvh=kvh, group=group),
        grid=(batch, nb),
        in_specs=[pl.BlockSpec(memory_space=pltpu.SMEM),
                  pl.BlockSpec((w, qd), lambda b, n: (cur(b, n), 0)),
                  pl.BlockSpec((w, kvd), lambda b, n: (prev(b, n), kcol)),
                  pl.BlockSpec((w, kvd), lambda b, n: (cur(b, n), kcol)),
                  pl.BlockSpec((w, kvd), lambda b, n: (prev(b, n), kcol + 1)),
                  pl.BlockSpec((w, kvd), lambda b, n: (cur(b, n), kcol + 1))],
        out_specs=[pl.BlockSpec((w, qd), lambda b, n: (cur(b, n), 0)), last_spec, last_spec],
        out_shape=[jax.ShapeDtypeStruct((batch * seq, qd), BF16),
                   jax.ShapeDtypeStruct((batch, w, kvd), F32),
                   jax.ShapeDtypeStruct((batch, w, kvd), F32)],
        compiler_params=_params("parallel", "arbitrary"),
    )(sinks, qkv, qkv, qkv, qkv, qkv)


def _swa_decode_kernel(sinks_ref, x_ref, kp_ref, vp_ref, nk_ref, nv_ref, o_ref, *, kvh, group):
    bb, w, kvd = kp_ref.shape
    hd = kvd // kvh
    qd = kvh * group * hd
    scale = hd ** -0.5
    row = lax.broadcasted_iota(jnp.int32, (w, kvd), 0)
    gi = lax.broadcasted_iota(jnp.int32, (group, 1), 0)
    for bi in range(bb):
        x = x_ref[bi:bi + 1, :]
        kc = jnp.where(row == w - 1, x[:, qd:qd + kvd], pltpu.roll(kp_ref[bi], w - 1, axis=0))
        vc = jnp.where(row == w - 1, x[:, qd + kvd:], pltpu.roll(vp_ref[bi], w - 1, axis=0))
        nk_ref[bi] = kc
        nv_ref[bi] = vc
        kcb = kc.astype(BF16)
        vcb = vc.astype(BF16)
        pieces = []
        for kv in range(kvh):
            c0 = kv * group * hd
            qg = jnp.concatenate([x[:, c0 + g * hd:c0 + (g + 1) * hd] for g in range(group)], axis=0)
            qg = (qg * scale).astype(BF16)
            s = lax.dot_general(qg, kcb[:, kv * hd:(kv + 1) * hd], _NT, preferred_element_type=F32)
            sk = jnp.zeros((group, 1), F32)
            for g in range(group):
                sk = jnp.where(gi == g, sinks_ref[kv * group + g], sk)
            og = _softmax_sink_pv(s, sk, vcb[:, kv * hd:(kv + 1) * hd])
            pieces += [og[g:g + 1, :] for g in range(group)]
        o_ref[bi:bi + 1, :] = jnp.concatenate(pieces, axis=1)


def _swa_decode(qkv, sinks, k_past, v_past, row0, bb=8):
    b, w, kvh, hd = k_past.shape
    kvd = kvh * hd
    qd = qkv.shape[1] - 2 * kvd
    group = qd // kvd
    bb = min(bb, b)
    blk0 = row0 // bb
    cache_spec = pl.BlockSpec((bb, w, kvd), lambda i: (i, 0, 0))
    nk, nv, o = pl.pallas_call(
        functools.partial(_swa_decode_kernel, kvh=kvh, group=group),
        grid=(b // bb,),
        in_specs=[pl.BlockSpec(memory_space=pltpu.SMEM),
                  pl.BlockSpec((bb, qkv.shape[1]), lambda i: (blk0 + i, 0)),
                  cache_spec, cache_spec],
        out_specs=[cache_spec, cache_spec, pl.BlockSpec((bb, qd), lambda i: (i, 0))],
        out_shape=[jax.ShapeDtypeStruct((b, w, kvd), F32), jax.ShapeDtypeStruct((b, w, kvd), F32),
                   jax.ShapeDtypeStruct((b, qd), F32)],
        compiler_params=_params("parallel"),
    )(sinks, qkv, k_past.reshape(b, w, kvd), v_past.reshape(b, w, kvd))
    return o, nk.reshape(b, w, kvh, hd), nv.reshape(b, w, kvh, hd)


def _log_decay(gl, wgu, bg):
    x = jnp.dot(gl.astype(BF16), wgu.astype(BF16), preferred_element_type=F32) + bg
    return (jnp.minimum(x, 0.0) - jnp.log(1.0 + jnp.exp(-jnp.abs(x)))) * (1.0 / GLA_TAU)


def _gla_out(o, ng, r):
    return _rms(o, ng) * (r * _sigmoid(r))


def _block_ref_row(x, row, s):
    c, d = x.shape
    if 2 * s >= 8:
        nblk = c // (2 * s)
        ref = x.reshape(nblk, 2 * s, d)[:, s - 1:s, :]
        return jnp.broadcast_to(ref, (nblk, 2 * s, d)).reshape(c, d)
    pos = row & (2 * s - 1)
    out = x
    for delta in range(s - 1, -s - 1, -1):
        if delta != 0:
            out = jnp.where(pos == s - 1 - delta, pltpu.roll(x, (c - delta) % c, axis=0), out)
    return out


def _gla_prompt_kernel(q_ref, k_ref, v_ref, r_ref, gl_ref, wgu_ref, bg_ref, ng_ref, o_ref, s_ref):
    c, dkh = q_ref.shape

    @pl.when(pl.program_id(2) == 0)
    def _():
        s_ref[...] = jnp.zeros_like(s_ref)

    g = _log_decay(gl_ref[...], wgu_ref[...], bg_ref[...])
    row = lax.broadcasted_iota(jnp.int32, (c, dkh), 0)
    cum = g
    sh = 1
    while sh < c:
        cum = cum + jnp.where(row >= sh, pltpu.roll(cum, sh, axis=0), 0.0)
        sh *= 2
    q = q_ref[...] * (dkh ** -0.5)
    k = k_ref[...]
    vb = v_ref[...].astype(BF16)

    ri = lax.broadcasted_iota(jnp.int32, (c, c), 0)
    ci = lax.broadcasted_iota(jnp.int32, (c, c), 1)
    att = jnp.where(ri == ci, lax.dot_general(q.astype(BF16), k.astype(BF16), _NT, preferred_element_type=F32), 0.0)
    s = c // 2
    while s >= 1:
        cref = _block_ref_row(cum, row, s)
        e = jnp.exp(jnp.where((row & s) != 0, cum - cref, cref - cum))
        a_s = lax.dot_general((q * e).astype(BF16), (k * e).astype(BF16), _NT, preferred_element_type=F32)
        pair = (((ri ^ ci) >> (s.bit_length() - 1)) == 1) & ((ri & s) != 0)
        att = att + jnp.where(pair, a_s, 0.0)
        s //= 2

    st = s_ref[...]
    o = jnp.dot((q * jnp.exp(cum)).astype(BF16), st.astype(BF16), preferred_element_type=F32)
    o = o + jnp.dot(att.astype(BF16), vb, preferred_element_type=F32)
    o_ref[...] = _gla_out(o, ng_ref[...], r_ref[...]).astype(o_ref.dtype)

    last = cum[c - 1:c, :]
    kd_t = (k * jnp.exp(last - cum)).T.astype(BF16)
    e_col = jnp.exp(cum.T[:, c - 1:c])
    s_ref[...] = e_col * st + jnp.dot(kd_t, vb, preferred_element_type=F32)


def _gla_prompt(proj, gl, w_gate_up, b_gate, norm_g, batch, seq, t_out):
    h = GLA_HEADS
    dk = w_gate_up.shape[-1]
    dkh = dk // h
    dv = (proj.shape[1] - 2 * dk) // 2
    dvh = dv // h
    c = GLA_CHUNK
    nc = seq // c

    def rows(b, hh, i):
        return b * nc + i

    return pl.pallas_call(
        _gla_prompt_kernel,
        grid=(batch, h, nc),
        in_specs=[pl.BlockSpec((c, dkh), lambda b, hh, i: (rows(b, hh, i), hh)),
                  pl.BlockSpec((c, dkh), lambda b, hh, i: (rows(b, hh, i), h + hh)),
                  pl.BlockSpec((c, dvh), lambda b, hh, i: (rows(b, hh, i), 2 * dk // dvh + hh)),
                  pl.BlockSpec((c, dvh), lambda b, hh, i: (rows(b, hh, i), (2 * dk + dv) // dvh + hh)),
                  pl.BlockSpec((c, GLA_GATE_RANK), lambda b, hh, i: (rows(b, hh, i), 0)),
                  pl.BlockSpec((None, GLA_GATE_RANK, dkh), lambda b, hh, i: (0, 0, hh)),
                  pl.BlockSpec((1, dkh), lambda b, hh, i: (0, hh)),
                  pl.BlockSpec((1, dvh), lambda b, hh, i: (0, 0))],
        out_specs=[pl.BlockSpec((c, dvh), lambda b, hh, i: (rows(b, hh, i), hh)),
                   pl.BlockSpec((None, None, dkh, dvh), lambda b, hh, i: (b, hh, 0, 0))],
        out_shape=[jax.ShapeDtypeStruct((t_out, dv), BF16),
                   jax.ShapeDtypeStruct((batch, h, dkh, dvh), F32)],
        compiler_params=_params("parallel", "parallel", "arbitrary"),
    )(proj, proj, proj, proj, gl, w_gate_up, b_gate, norm_g)


def _gla_gate_kernel(gl_ref, wgu_ref, bg_ref, o_ref):
    o_ref[...] = _log_decay(gl_ref[...], wgu_ref[...], bg_ref[...])


def _gla_gate(gl, w_gate_up, b_gate, row0, rows):
    dk = w_gate_up.shape[-1]
    return pl.pallas_call(
        _gla_gate_kernel,
        grid=(1,),
        in_specs=[pl.BlockSpec((rows, GLA_GATE_RANK), lambda i: (row0 // rows, 0)),
                  pl.BlockSpec((None, GLA_GATE_RANK, dk), lambda i: (0, 0, 0)),
                  pl.BlockSpec((1, dk), lambda i: (0, 0))],
        out_specs=pl.BlockSpec((rows, dk), lambda i: (0, 0)),
        out_shape=jax.ShapeDtypeStruct((rows, dk), F32),
        compiler_params=_params("arbitrary"),
    )(gl, w_gate_up, b_gate)


def _gla_decode_kernel(q_ref, v_ref, r_ref, kc_ref, gc_ref, ng_ref, s_ref, o_ref, ns_ref):
    dkh = q_ref.shape[-1]
    sn = jnp.exp(gc_ref[...]) * s_ref[...] + kc_ref[...] * v_ref[...]
    ns_ref[...] = sn
    q = jnp.broadcast_to(q_ref[...] * (dkh ** -0.5), (8, dkh)).astype(BF16)
    o = jnp.dot(q, sn.astype(BF16), preferred_element_type=F32)[0:1, :]
    o_ref[...] = _gla_out(o, ng_ref[...], r_ref[...])


def _gla_decode(proj_s, log_a, norm_g, state):
    b, h, dkh, dvh = state.shape
    dk, dv = h * dkh, h * dvh
    k_col = proj_s[:, 0, dk:2 * dk].reshape(b, h, dkh, 1)
    g_col = log_a.reshape(b, h, dkh, 1)
    col_spec = pl.BlockSpec((None, None, dkh, 1), lambda i, hh: (i, hh, 0, 0))
    st_spec = pl.BlockSpec((None, None, dkh, dvh), lambda i, hh: (i, hh, 0, 0))
    return pl.pallas_call(
        _gla_decode_kernel,
        grid=(b, h),
        in_specs=[pl.BlockSpec((None, 1, dkh), lambda i, hh: (i, 0, hh)),
                  pl.BlockSpec((None, 1, dvh), lambda i, hh: (i, 0, 2 * dk // dvh + hh)),
                  pl.BlockSpec((None, 1, dvh), lambda i, hh: (i, 0, (2 * dk + dv) // dvh + hh)),
                  col_spec, col_spec,
                  pl.BlockSpec((1, dvh), lambda i, hh: (0, 0)),
                  st_spec],
        out_specs=[pl.BlockSpec((None, 1, dvh), lambda i, hh: (i, 0, hh)), st_spec],
        out_shape=[jax.ShapeDtypeStruct((b, 1, dv), F32), jax.ShapeDtypeStruct(state.shape, F32)],
        compiler_params=_params("parallel", "parallel"),
    )(proj_s, proj_s, proj_s, k_col, g_col, norm_g, state)


def _pack_bf16_halves(y):
    half = y.shape[1] // 2
    bits = lax.bitcast_convert_type(y.astype(BF16).astype(F32), jnp.uint32)
    return (bits[:, :half] >> 16) | (bits[:, half:] & jnp.uint32(0xFFFF0000))


def _unpack_bf16_halves(u):
    lo = lax.bitcast_convert_type(u << 16, F32)
    hi = lax.bitcast_convert_type(u & jnp.uint32(0xFFFF0000), F32)
    return jnp.concatenate([lo, hi], axis=1).astype(BF16)


def _for_rows(n, fn, unroll=8):
    def group(j, c):
        for u in range(unroll):
            fn(j * unroll + u)
        return c

    def single(r, c):
        fn(r)
        return c

    full = n // unroll
    lax.fori_loop(0, full, group, 0)
    lax.fori_loop(full * unroll, n, single, 0)


def _moe_kernel(ie_ref, ir_ref, in_ref, tok_ref, dst_ref, x_hbm, wg_ref, wu_ref, wd_ref, y_hbm,
                xbuf, ybuf, wgb, wub, wdb, gsem, ssem):
    i = pl.program_id(0)
    f = pl.program_id(1)
    ni = pl.num_programs(0)
    nf = pl.num_programs(1)
    n = in_ref[i]
    slot = i % 2
    rb = ybuf.shape[0]
    sub = MOE_SUB_BLOCK if rb % MOE_SUB_BLOCK == 0 else rb

    def gather(item, sl, r):
        src = x_hbm.at[pl.ds(tok_ref[ir_ref[item] + r], 1)]
        return pltpu.make_async_copy(src, xbuf.at[sl, pl.ds(r, 1)], gsem.at[sl])

    def scatter(item, r):
        dst = y_hbm.at[pl.ds(dst_ref[ir_ref[item] + r], 1)]
        return pltpu.make_async_copy(ybuf.at[pl.ds(r, 1)], dst, ssem)

    @pl.when(f == 0)
    def _():
        @pl.when(i == 0)
        def _():
            xbuf[...] = jnp.zeros_like(xbuf)
            _for_rows(n, lambda r: gather(0, 0, r).start())

        nxt = jnp.minimum(i + 1, ni - 1)
        n_nxt = jnp.where(i + 1 < ni, in_ref[nxt], 0)
        _for_rows(n_nxt, lambda r: gather(nxt, 1 - slot, r).start())
        _for_rows(n, lambda r: gather(i, slot, r).wait())
        prv = jnp.maximum(i - 1, 0)
        n_prv = jnp.where(i > 0, in_ref[prv], 0)
        _for_rows(n_prv, lambda r: scatter(prv, r).wait())

    @pl.when(n > 0)
    def _():
        _cast_rows(wg_ref, wgb)
        _cast_rows(wu_ref, wub)
        _cast_rows(wd_ref, wdb)

    for sb in range(rb // sub):
        sl = slice(sb * sub, (sb + 1) * sub)

        @pl.when(sb * sub < n)
        def _():
            xs = _unpack_bf16_halves(xbuf[slot, sl, :])
            hg = jnp.dot(xs, wgb[...], preferred_element_type=F32)
            hu = jnp.dot(xs, wub[...], preferred_element_type=F32)
            act = (hg * _sigmoid(hg) * hu).astype(BF16)
            y = jnp.dot(act, wdb[...], preferred_element_type=F32)

            @pl.when(f == 0)
            def _():
                ybuf[sl, :] = y

            @pl.when(f > 0)
            def _():
                ybuf[sl, :] += y

    @pl.when(f == nf - 1)
    def _():
        _for_rows(n, lambda r: scatter(i, r).start())

        @pl.when(i == ni - 1)
        def _():
            _for_rows(n, lambda r: scatter(i, r).wait())


def _moe_experts(xn, eid, w_gate, w_up, w_down, layer):
    t, d = xn.shape[0], xn.shape[1] * 2
    n_exp, _, dff = w_gate.shape[1:]
    a = t * MOE_TOP_K
    rb = min(MOE_ROW_BLOCK, -(-a // 8) * 8)
    fc = min(MOE_FF_TILE, dff)
    nf = dff // fc
    n_items = n_exp + a // rb

    flat = eid.reshape(-1)
    order = jnp.argsort(flat, stable=True).astype(jnp.int32)
    counts = jnp.sum((flat[:, None] == jnp.arange(n_exp, dtype=jnp.int32)[None, :]).astype(jnp.int32), axis=0)
    starts = jnp.cumsum(counts) - counts
    per_e = (counts + rb - 1) // rb
    item_end = jnp.cumsum(per_e)
    item_start = item_end - per_e
    ii = jnp.arange(n_items, dtype=jnp.int32)
    valid = ii < item_end[-1]
    e_of = jnp.minimum(jnp.searchsorted(item_end, ii, side="right").astype(jnp.int32), n_exp - 1)
    local = ii - item_start[e_of]
    item_rows = jnp.where(valid, jnp.clip(counts[e_of] - local * rb, 0, rb), 0).astype(jnp.int32)
    item_row0 = jnp.where(valid, starts[e_of] + local * rb, 0).astype(jnp.int32)
    e_last = jnp.max(jnp.where(valid, e_of, 0))
    item_e = jnp.where(valid, e_of, e_last).astype(jnp.int32)
    tok = order // MOE_TOP_K
    dst = (order % MOE_TOP_K) * t + tok

    def w_in_map(i, f, ie, ir, inn, tk, ds_):
        return (layer, ie[i], 0, jnp.where(inn[i] > 0, f, nf - 1))

    def w_out_map(i, f, ie, ir, inn, tk, ds_):
        return (layer, ie[i], jnp.where(inn[i] > 0, f, nf - 1), 0)

    grid_spec = pltpu.PrefetchScalarGridSpec(
        num_scalar_prefetch=5,
        grid=(n_items, nf),
        in_specs=[pl.BlockSpec(memory_space=pl.ANY),
                  pl.BlockSpec((None, None, d, fc), w_in_map),
                  pl.BlockSpec((None, None, d, fc), w_in_map),
                  pl.BlockSpec((None, None, fc, d), w_out_map)],
        out_specs=pl.BlockSpec(memory_space=pl.ANY),
        scratch_shapes=[pltpu.VMEM((2, rb, d // 2), jnp.uint32), pltpu.VMEM((rb, d), F32),
                        pltpu.VMEM((d, fc), BF16), pltpu.VMEM((d, fc), BF16), pltpu.VMEM((fc, d), BF16),
                        pltpu.SemaphoreType.DMA((2,)), pltpu.SemaphoreType.DMA(())],
    )
    y = pl.pallas_call(
        _moe_kernel,
        grid_spec=grid_spec,
        out_shape=jax.ShapeDtypeStruct((a, d), F32),
        compiler_params=_params("arbitrary", "arbitrary"),
    )(item_e, item_row0, item_rows, tok, dst, xn, w_gate, w_up, w_down)
    return y.reshape(MOE_TOP_K, t, d)


def _combine_norm_kernel(h_ref, y0_ref, y1_ref, gate_ref, g_ref, ho_ref, a_ref):
    h = h_ref[...] + (y0_ref[...] * gate_ref[:, 0:1] + y1_ref[...] * gate_ref[:, 1:2])
    ho_ref[...] = h
    a_ref[...] = _rms(h, g_ref[...]).astype(a_ref.dtype)


def _combine_norm(h, y2, gates, g, layer, tm=ROW_TILE // 2):
    t, d = h.shape
    return pl.pallas_call(
        _combine_norm_kernel,
        grid=_row_grid(t, tm),
        in_specs=[pl.BlockSpec((tm, d), lambda i: (i, 0)),
                  pl.BlockSpec((None, tm, d), lambda i: (0, i, 0)),
                  pl.BlockSpec((None, tm, d), lambda i: (1, i, 0)),
                  pl.BlockSpec((tm, MOE_TOP_K), lambda i: (i, 0)),
                  pl.BlockSpec((None, 1, d), lambda i: (layer, 0, 0))],
        out_specs=[pl.BlockSpec((tm, d), lambda i: (i, 0)), pl.BlockSpec((tm, d), lambda i: (i, 0))],
        out_shape=[jax.ShapeDtypeStruct((t, d), F32), jax.ShapeDtypeStruct((t, d), BF16)],
        compiler_params=_params("parallel"),
    )(h, y2, y2, gates, g)


def kernel(x_prompt, x_sample, cache_swa_k, cache_swa_v, state_gla, p_prompt, p_sample, ln_mix, ln_ffn, ln_ple, ln_final, swa_w_in, swa_sinks, swa_w_out, gla_w_in, gla_w_gate_up, gla_b_gate, gla_norm, gla_w_out, moe_w_group, moe_b_group, moe_w_expert, moe_b_expert, moe_w_gate, moe_w_up, moe_w_down, ple_w_proj, ple_w_gate):
    batch, seq, d = x_prompt.shape
    dec = x_sample.shape[0]
    assert x_sample.shape[1] == 1
    tp = batch * seq
    t = tp + dec
    depth = ln_mix.shape[0]
    kvh = cache_swa_k.shape[3]
    kvd = kvh * SWA_HEAD_DIM
    dk = gla_w_gate_up.shape[-1]
    dv = (gla_w_in.shape[-1] - GLA_GATE_RANK - 2 * dk) // 2

    x_p, x_s = x_prompt.reshape(tp, d), x_sample.reshape(dec, d)
    h = None
    p_all = jnp.concatenate([p_prompt.reshape(depth, tp, -1), p_sample.reshape(depth, dec, -1)], axis=1)
    pad = ROUTER_LANES - MOE_GROUPS - MOE_EXPERTS
    w_router = jnp.concatenate([moe_w_group, moe_w_expert, jnp.zeros((depth, d, pad), F32)], axis=-1)
    b_router = jnp.concatenate([moe_b_group, moe_b_expert, jnp.zeros((depth, pad), F32)], axis=-1)
    ln_mix, ln_ffn, ln_ple = (g.reshape(depth, 1, d) for g in (ln_mix, ln_ffn, ln_ple))

    new_k_p, new_v_p, new_s_p, new_k_s, new_v_s, new_s_s = [], [], [], [], [], []
    for i in range(depth):
        j = i // 2
        if i % 2 == 0:
            if h is None:
                qkv = _matmul(_rmsnorm(x_p, ln_mix, i), swa_w_in, j, swa_w_in.shape[-1],
                              a_tail=_rmsnorm(x_s, ln_mix, i))
            else:
                qkv = _matmul(_rmsnorm(h, ln_mix, i), swa_w_in, j, swa_w_in.shape[-1])
            o, nk_p, nv_p = _swa_prompt(qkv, swa_sinks[j], batch, seq, kvh)
            o_s, nk, nv = _swa_decode(qkv, swa_sinks[j], cache_swa_k[j], cache_swa_v[j], tp)
            new_k_p.append(nk_p.reshape(batch, WINDOW, kvh, SWA_HEAD_DIM))
            new_v_p.append(nv_p.reshape(batch, WINDOW, kvh, SWA_HEAD_DIM))
            new_k_s.append(nk)
            new_v_s.append(nv)
            if h is None:
                h = _matmul(o, swa_w_out, j, d, mode="res", res=x_p, res_tail=x_s, a_tail=o_s.astype(BF16))
            else:
                h = _matmul(o, swa_w_out, j, d, mode="res", res=h, a_tail=o_s.astype(BF16))
        else:
            w_gl = gla_w_in[j][:, 2 * dk + 2 * dv:]
            a, gl = _rmsnorm_proj(h, ln_mix, i, w_gl)
            proj = _matmul(a, gla_w_in, j, 2 * dk + 2 * dv)
            o, s_p = _gla_prompt(proj, gl, gla_w_gate_up[j:j + 1], gla_b_gate[j:j + 1], gla_norm[j:j + 1], batch, seq, tp)
            log_a = _gla_gate(gl, gla_w_gate_up[j:j + 1], gla_b_gate[j:j + 1], tp, dec)
            o_s, s_s = _gla_decode(proj[tp:].reshape(dec, 1, -1), log_a, gla_norm[j:j + 1], state_gla[j])
            new_s_p.append(s_p)
            new_s_s.append(s_s)
            h = _matmul(o, gla_w_out, j, d, mode="res", res=h, a_tail=o_s.reshape(dec, dv).astype(BF16))
        xn, eid, gates = _rmsnorm_router(h, ln_ffn, i, w_router[i], b_router[i:i + 1])
        y2 = _moe_experts(xn, eid, moe_w_gate, moe_w_up, moe_w_down, i)
        h, a3 = _combine_norm(h, y2, gates, ln_ple, i)
        h = _matmul(a3, ple_w_gate, i, d, mode="ple", res=h, p=p_all, wp=ple_w_proj)

    g_fin = ln_final.reshape(1, 1, d)
    y_p = _rmsnorm(h, g_fin, 0, out_dtype=F32, rows=tp)
    y_s = _rmsnorm(h, g_fin, 0, out_dtype=F32, rows=dec, row_block0=tp // dec, tm=dec)
    return (y_p.reshape(batch, seq, d), y_s.reshape(dec, 1, d),
            jnp.stack(new_k_p), jnp.stack(new_v_p), jnp.stack(new_s_p),
            jnp.stack(new_k_s), jnp.stack(new_v_s), jnp.stack(new_s_s))
```

```python
import functools

import jax
import jax.numpy as jnp
from jax import lax
from jax.experimental import pallas as pl
from jax.experimental.pallas import tpu as pltpu

F32 = jnp.float32
BF16 = jnp.bfloat16

EPS = 1e-6
WINDOW = 128
SWA_HEAD_DIM = 128
GLA_HEADS = 4
GLA_GATE_RANK = 16
GLA_TAU = 16.0
GLA_CHUNK = 128
MOE_GROUPS = 8
MOE_EXPERTS_PER_GROUP = 8
MOE_EXPERTS = MOE_GROUPS * MOE_EXPERTS_PER_GROUP
MOE_TOP_K = 2
ROUTER_LANES = 128

V7X_VMEM_LIMIT_BYTES = 60 * 1024 * 1024
ROW_TILE = 512
MM_ROW_TILE = 1024
COL_TILE = 512
MOE_ROW_BLOCK = 512
MOE_ROW_SIZES = (256, 384, 512)
MOE_FF_TILE = 256
CAST_ROWS = 256


def _params(*sem):
    return pltpu.CompilerParams(dimension_semantics=sem, vmem_limit_bytes=V7X_VMEM_LIMIT_BYTES)


def _sigmoid(x):
    return 1.0 / (1.0 + jnp.exp(-x))


def _cast_rows(src_ref, dst_ref):
    rows = src_ref.shape[0]
    step = min(CAST_ROWS, rows)

    def body(i, c):
        r = pl.multiple_of(i * step, step)
        dst_ref[pl.ds(r, step), :] = src_ref[pl.ds(r, step), :].astype(BF16)
        return c

    lax.fori_loop(0, rows // step, body, 0)


def _rms(x, g):
    ms = jnp.mean(x * x, axis=-1, keepdims=True)
    return x * lax.rsqrt(ms + EPS) * g


def _norm_kernel(h_ref, g_ref, o_ref):
    o_ref[...] = _rms(h_ref[...], g_ref[...]).astype(o_ref.dtype)


def _norm_proj_kernel(h_ref, g_ref, w_ref, o_ref, s_ref):
    y = _rms(h_ref[...], g_ref[...]).astype(BF16)
    o_ref[...] = y
    s_ref[...] = jnp.dot(y, w_ref[...].astype(BF16), preferred_element_type=F32)


def _route(logits):
    lane = lax.broadcasted_iota(jnp.int32, logits.shape, 1)
    neg = jnp.float32(-jnp.inf)
    is_g = lane < MOE_GROUPS
    lg = jnp.where(is_g, logits, neg)
    mg = jnp.max(lg, axis=-1, keepdims=True)
    g_sel = jnp.min(jnp.where(lg == mg, lane, ROUTER_LANES), axis=-1, keepdims=True)
    p_g = 1.0 / jnp.sum(jnp.where(is_g, jnp.exp(lg - mg), 0.0), axis=-1, keepdims=True)
    lo = MOE_GROUPS + g_sel * MOE_EXPERTS_PER_GROUP
    in_grp = (lane >= lo) & (lane < lo + MOE_EXPERTS_PER_GROUP)
    le = jnp.where(in_grp, logits, neg)
    v0 = jnp.max(le, axis=-1, keepdims=True)
    i0 = jnp.min(jnp.where(le == v0, lane, ROUTER_LANES), axis=-1, keepdims=True)
    le1 = jnp.where(lane == i0, neg, le)
    v1 = jnp.max(le1, axis=-1, keepdims=True)
    i1 = jnp.min(jnp.where(le1 == v1, lane, ROUTER_LANES), axis=-1, keepdims=True)
    e1 = jnp.exp(v1 - v0)
    g0 = p_g / (1.0 + e1)
    g1 = p_g * e1 / (1.0 + e1)
    return i0 - MOE_GROUPS, i1 - MOE_GROUPS, g0, g1


def _norm_router_kernel(h_ref, g_ref, w_ref, b_ref, o_ref, eid_ref, gate_ref):
    y = _rms(h_ref[...], g_ref[...])
    o_ref[...] = _pack_bf16_halves(y)
    w = w_ref[...]
    exact = jnp.dot(y, w, preferred_element_type=F32, precision=lax.Precision.HIGHEST)
    rounded = jnp.dot(y.astype(BF16), w.astype(BF16), preferred_element_type=F32)
    lane = lax.broadcasted_iota(jnp.int32, exact.shape, 1)
    e0, e1, g0, g1 = _route(jnp.where(lane < MOE_GROUPS, rounded, exact) + b_ref[...])
    eid_ref[:, 0:1] = e0
    eid_ref[:, 1:2] = e1
    gate_ref[:, 0:1] = g0
    gate_ref[:, 1:2] = g1


def _row_grid(t, tm):
    return (pl.cdiv(t, tm),)


def _rmsnorm(h, g, layer, out_dtype=BF16, rows=None, row_block0=0, tm=ROW_TILE):
    t, d = h.shape
    rows = t if rows is None else rows
    tm = min(tm, rows)
    return pl.pallas_call(
        _norm_kernel,
        grid=_row_grid(rows, tm),
        in_specs=[pl.BlockSpec((tm, d), lambda i: (row_block0 + i, 0)),
                  pl.BlockSpec((None, 1, d), lambda i: (layer, 0, 0))],
        out_specs=pl.BlockSpec((tm, d), lambda i: (i, 0)),
        out_shape=jax.ShapeDtypeStruct((rows, d), out_dtype),
        compiler_params=_params("parallel"),
    )(h, g)


def _rmsnorm_proj(h, g, layer, w_small, tm=ROW_TILE):
    t, d = h.shape
    ns = w_small.shape[1]
    return pl.pallas_call(
        _norm_proj_kernel,
        grid=_row_grid(t, tm),
        in_specs=[pl.BlockSpec((tm, d), lambda i: (i, 0)),
                  pl.BlockSpec((None, 1, d), lambda i: (layer, 0, 0)),
                  pl.BlockSpec((d, ns), lambda i: (0, 0))],
        out_specs=[pl.BlockSpec((tm, d), lambda i: (i, 0)),
                   pl.BlockSpec((tm, ns), lambda i: (i, 0))],
        out_shape=[jax.ShapeDtypeStruct((t, d), BF16), jax.ShapeDtypeStruct((t, ns), F32)],
        compiler_params=_params("parallel"),
    )(h, g, w_small)


def _rmsnorm_router(h, g, layer, w_router, b_router, tm=ROW_TILE // 2):
    t, d = h.shape
    return pl.pallas_call(
        _norm_router_kernel,
        grid=_row_grid(t, tm),
        in_specs=[pl.BlockSpec((tm, d), lambda i: (i, 0)),
                  pl.BlockSpec((None, 1, d), lambda i: (layer, 0, 0)),
                  pl.BlockSpec((d, ROUTER_LANES), lambda i: (0, 0)),
                  pl.BlockSpec((1, ROUTER_LANES), lambda i: (0, 0))],
        out_specs=[pl.BlockSpec((tm, d // 2), lambda i: (i, 0)),
                   pl.BlockSpec((tm, MOE_TOP_K), lambda i: (i, 0)),
                   pl.BlockSpec((tm, MOE_TOP_K), lambda i: (i, 0))],
        out_shape=[jax.ShapeDtypeStruct((t, d // 2), jnp.uint32),
                   jax.ShapeDtypeStruct((t, MOE_TOP_K), jnp.int32),
                   jax.ShapeDtypeStruct((t, MOE_TOP_K), F32)],
        compiler_params=_params("parallel"),
    )(h, g, w_router, b_router)


def _mm_kernel(*refs, mode, tail, split_a, split_r):
    a_ref, refs = refs[0], refs[1:]
    at_ref = a_ref
    if split_a:
        at_ref, refs = refs[0], refs[1:]
    w_ref, refs = refs[0], refs[1:]
    r_ref = rt_ref = None
    if mode != "plain":
        r_ref, refs = refs[0], refs[1:]
        rt_ref = r_ref
        if split_r:
            rt_ref, refs = refs[0], refs[1:]
    if mode == "ple":
        p_ref, wp_ref, o_ref, wb_ref, wpb_ref = refs
    else:
        o_ref, wb_ref = refs
    m = pl.program_id(1)

    @pl.when(m == 0)
    def _():
        _cast_rows(w_ref, wb_ref)
        if mode == "ple":
            _cast_rows(wp_ref, wpb_ref)

    def tile(src_ref, res_ref, rows):
        acc = jnp.dot(src_ref[rows, :], wb_ref[...], preferred_element_type=F32)
        if mode == "plain":
            o_ref[rows, :] = acc.astype(o_ref.dtype)
        elif mode == "res":
            o_ref[rows, :] = res_ref[rows, :] + acc
        else:
            pp = jnp.dot(p_ref[rows, :].astype(BF16), wpb_ref[...], preferred_element_type=F32)
            o_ref[rows, :] = res_ref[rows, :] + pp * _sigmoid(acc)

    if tail == 0:
        tile(a_ref, r_ref, slice(None))
    else:
        last = pl.num_programs(1) - 1
        pl.when(m < last)(lambda: tile(a_ref, r_ref, slice(None)))
        pl.when(m == last)(lambda: tile(at_ref, rt_ref, slice(0, tail)))


def _matmul(a, w, layer, n_cols, mode="plain", res=None, p=None, wp=None, a_tail=None, res_tail=None,
            out_dtype=F32, tm=MM_ROW_TILE, tn=COL_TILE):
    k = a.shape[1]
    tm = min(tm, a.shape[0] // 256 * 256)
    split_a = a_tail is not None
    split_r = res_tail is not None
    if split_a:
        assert a.shape[0] % tm == 0 and a_tail.shape[0] < tm
        t, tail = a.shape[0] + a_tail.shape[0], a_tail.shape[0]
    else:
        t, tail = a.shape[0], a.shape[0] % tm
    if split_r:
        assert res.shape[0] == t - tail and res_tail.shape[0] == tail
    n_full = t // tm

    def full_rows(n, m):
        return jnp.minimum(m, n_full - 1)

    tn = min(tn, n_cols)
    assert n_cols % tn == 0
    grid = (n_cols // tn, pl.cdiv(t, tm))
    if split_a:
        in_specs = [pl.BlockSpec((tm, k), lambda n, m: (full_rows(n, m), 0)),
                    pl.BlockSpec((tail, k), lambda n, m: (0, 0))]
        args = [a, a_tail]
    else:
        in_specs = [pl.BlockSpec((tm, k), lambda n, m: (m, 0))]
        args = [a]
    in_specs.append(pl.BlockSpec((None, k, tn), lambda n, m: (layer, 0, n)))
    args.append(w)
    scratch = [pltpu.VMEM((k, tn), BF16)]
    if mode in ("res", "ple"):
        if split_r:
            in_specs += [pl.BlockSpec((tm, tn), lambda n, m: (full_rows(n, m), n)),
                         pl.BlockSpec((tail, tn), lambda n, m: (0, n))]
            args += [res, res_tail]
        else:
            in_specs.append(pl.BlockSpec((tm, tn), lambda n, m: (m, n)))
            args.append(res)
    if mode == "ple":
        kp = p.shape[-1]
        in_specs += [pl.BlockSpec((None, tm, kp), lambda n, m: (layer, m, 0)),
                     pl.BlockSpec((None, kp, tn), lambda n, m: (layer, 0, n))]
        args += [p, wp]
        scratch.append(pltpu.VMEM((kp, tn), BF16))
    return pl.pallas_call(
        functools.partial(_mm_kernel, mode=mode, tail=tail, split_a=split_a, split_r=split_r),
        grid=grid,
        in_specs=in_specs,
        out_specs=pl.BlockSpec((tm, tn), lambda n, m: (m, n)),
        out_shape=jax.ShapeDtypeStruct((t, n_cols), out_dtype),
        scratch_shapes=scratch,
        compiler_params=_params("arbitrary", "arbitrary"),
    )(*args)


def _softmax_sink_pv(s, sk, vb):
    m = jnp.maximum(jnp.max(s, axis=-1, keepdims=True), sk)
    p = jnp.exp(s - m)
    den = jnp.sum(p, axis=-1, keepdims=True) + jnp.exp(sk - m)
    return jnp.dot((p / den).astype(BF16), vb, preferred_element_type=F32)


_NT = (((1,), (1,)), ((), ()))


def _swa_prompt_kernel(sinks_ref, q_ref, kp_ref, kc_ref, vp_ref, vc_ref, o_ref, nk_ref, nv_ref, *, kvh, group):
    n = pl.program_id(1)
    w, kvd = kc_ref.shape
    hd = kvd // kvh
    kk = jnp.concatenate([kp_ref[...], kc_ref[...]], axis=0).astype(BF16)
    vv = jnp.concatenate([vp_ref[...], vc_ref[...]], axis=0).astype(BF16)
    qi = lax.broadcasted_iota(jnp.int32, (w, 2 * w), 0)
    kj = lax.broadcasted_iota(jnp.int32, (w, 2 * w), 1)
    mask = (kj > qi) & (kj <= qi + w) & (kj >= jnp.where(n > 0, 0, w))
    scale = hd ** -0.5
    for kv in range(kvh):
        kh = kk[:, kv * hd:(kv + 1) * hd]
        vh = vv[:, kv * hd:(kv + 1) * hd]
        for g in range(group):
            c0 = (kv * group + g) * hd
            qg = (q_ref[:, c0:c0 + hd] * scale).astype(BF16)
            s = lax.dot_general(qg, kh, _NT, preferred_element_type=F32)
            s = jnp.where(mask, s, -jnp.inf)
            o = _softmax_sink_pv(s, sinks_ref[kv * group + g], vh)
            o_ref[:, c0:c0 + hd] = o.astype(o_ref.dtype)

    @pl.when(n == pl.num_programs(1) - 1)
    def _():
        nk_ref[...] = kc_ref[...]
        nv_ref[...] = vc_ref[...]


def _swa_prompt(qkv, sinks, batch, seq, kvh):
    hd, w = SWA_HEAD_DIM, WINDOW
    kvd = kvh * hd
    qd = qkv.shape[1] - 2 * kvd
    group = qd // kvd
    nb = seq // w
    kcol = qd // kvd

    def cur(b, n):
        return b * nb + n

    def prev(b, n):
        return b * nb + jnp.maximum(n - 1, 0)

    last_spec = pl.BlockSpec((None, None, w, kvd), lambda b, n: (0, b, 0, 0))
    return pl.pallas_call(
        functools.partial(_swa_prompt_kernel, kvh=kvh, group=group),
        grid=(batch, nb),
        in_specs=[pl.BlockSpec(memory_space=pltpu.SMEM),
                  pl.BlockSpec((w, qd), lambda b, n: (cur(b, n), 0)),
                  pl.BlockSpec((w, kvd), lambda b, n: (prev(b, n), kcol)),
                  pl.BlockSpec((w, kvd), lambda b, n: (cur(b, n), kcol)),
                  pl.BlockSpec((w, kvd), lambda b, n: (prev(b, n), kcol + 1)),
                  pl.BlockSpec((w, kvd), lambda b, n: (cur(b, n), kcol + 1))],
        out_specs=[pl.BlockSpec((w, qd), lambda b, n: (cur(b, n), 0)), last_spec, last_spec],
        out_shape=[jax.ShapeDtypeStruct((batch * seq, qd), BF16),
                   jax.ShapeDtypeStruct((1, batch, w, kvd), F32),
                   jax.ShapeDtypeStruct((1, batch, w, kvd), F32)],
        compiler_params=_params("parallel", "arbitrary"),
    )(sinks, qkv, qkv, qkv, qkv, qkv)


def _swa_decode_kernel(sk_ref, q_ref, x_ref, kp_ref, vp_ref, nk_ref, nv_ref, o_ref, *, kvh, group):
    bb, w, kvd = kp_ref.shape
    hd = kvd // kvh
    nh = kvh * group
    qd = nh * hd
    scale = hd ** -0.5
    row = lax.broadcasted_iota(jnp.int32, (w, kvd), 0)
    head_kv = lax.broadcasted_iota(jnp.int32, (nh, hd), 0) // group
    sk = sk_ref[...]
    for bi in range(bb):
        x = x_ref[bi:bi + 1, :]
        kc = jnp.where(row == w - 1, x[:, qd:qd + kvd], pltpu.roll(kp_ref[bi], w - 1, axis=0))
        vc = jnp.where(row == w - 1, x[:, qd + kvd:], pltpu.roll(vp_ref[bi], w - 1, axis=0))
        nk_ref[bi] = kc
        nv_ref[bi] = vc
        s = lax.dot_general((q_ref[bi] * scale).astype(BF16), kc.astype(BF16), _NT, preferred_element_type=F32)
        ov = _softmax_sink_pv(s, sk, vc.astype(BF16))
        o = jnp.zeros((nh, hd), F32)
        for kv in range(kvh):
            o = jnp.where(head_kv == kv, ov[:, kv * hd:(kv + 1) * hd], o)
        o_ref[bi] = o


def _swa_decode(qkv, sinks, k_past, v_past, layer, row0, bb=8):
    nl, b, w, kvh, hd = k_past.shape
    kvd = kvh * hd
    qd = qkv.shape[1] - 2 * kvd
    group = qd // kvd
    nh = kvh * group
    bb = min(bb, b)
    blk0 = row0 // bb
    q = qkv[row0:row0 + b, :qd].reshape(b, nh, 1, hd)
    own = (jnp.arange(nh)[:, None] // group == jnp.arange(kvh)[None, :])[None, :, :, None]
    q_blocks = jnp.where(own, q, 0.0).reshape(b, nh, kvd)
    in_spec = pl.BlockSpec((None, bb, w, kvd), lambda i: (layer, i, 0, 0))
    out_spec = pl.BlockSpec((None, bb, w, kvd), lambda i: (0, i, 0, 0))
    nk, nv, o = pl.pallas_call(
        functools.partial(_swa_decode_kernel, kvh=kvh, group=group),
        grid=(b // bb,),
        in_specs=[pl.BlockSpec((nh, 1), lambda i: (0, 0)),
                  pl.BlockSpec((bb, nh, kvd), lambda i: (i, 0, 0)),
                  pl.BlockSpec((bb, qkv.shape[1]), lambda i: (blk0 + i, 0)),
                  in_spec, in_spec],
        out_specs=[out_spec, out_spec, pl.BlockSpec((bb, nh, hd), lambda i: (i, 0, 0))],
        out_shape=[jax.ShapeDtypeStruct((1, b, w, kvd), F32), jax.ShapeDtypeStruct((1, b, w, kvd), F32),
                   jax.ShapeDtypeStruct((b, nh, hd), F32)],
        compiler_params=_params("parallel"),
    )(sinks.reshape(nh, 1), q_blocks, qkv, k_past.reshape(nl, b, w, kvd), v_past.reshape(nl, b, w, kvd))
    return o.reshape(b, qd), nk.reshape(1, b, w, kvh, hd), nv.reshape(1, b, w, kvh, hd)


def _log_decay(gl, wgu, bg):
    x = jnp.dot(gl.astype(BF16), wgu.astype(BF16), preferred_element_type=F32) + bg
    return (jnp.minimum(x, 0.0) - jnp.log(1.0 + jnp.exp(-jnp.abs(x)))) * (1.0 / GLA_TAU)


def _gla_out(o, ng, r):
    return _rms(o, ng) * (r * _sigmoid(r))


def _block_ref_row(x, row, s):
    c, d = x.shape
    if 2 * s >= 8:
        nblk = c // (2 * s)
        ref = x.reshape(nblk, 2 * s, d)[:, s - 1:s, :]
        return jnp.broadcast_to(ref, (nblk, 2 * s, d)).reshape(c, d)
    pos = row & (2 * s - 1)
    out = x
    for delta in range(s - 1, -s - 1, -1):
        if delta != 0:
            out = jnp.where(pos == s - 1 - delta, pltpu.roll(x, (c - delta) % c, axis=0), out)
    return out


def _gla_prompt_kernel(q_ref, k_ref, v_ref, r_ref, gl_ref, wgu_ref, bg_ref, ng_ref, o_ref, s_ref):
    c, dkh = q_ref.shape

    @pl.when(pl.program_id(2) == 0)
    def _():
        s_ref[...] = jnp.zeros_like(s_ref)

    g = _log_decay(gl_ref[...], wgu_ref[...], bg_ref[...])
    row = lax.broadcasted_iota(jnp.int32, (c, dkh), 0)
    cum = g
    sh = 1
    while sh < c:
        cum = cum + jnp.where(row >= sh, pltpu.roll(cum, sh, axis=0), 0.0)
        sh *= 2
    q = q_ref[...] * (dkh ** -0.5)
    k = k_ref[...]
    vb = v_ref[...].astype(BF16)

    ri = lax.broadcasted_iota(jnp.int32, (c, c), 0)
    ci = lax.broadcasted_iota(jnp.int32, (c, c), 1)
    att = jnp.where(ri == ci, lax.dot_general(q.astype(BF16), k.astype(BF16), _NT, preferred_element_type=F32), 0.0)
    s = c // 2
    while s >= 1:
        cref = _block_ref_row(cum, row, s)
        e = jnp.exp(jnp.where((row & s) != 0, cum - cref, cref - cum))
        a_s = lax.dot_general((q * e).astype(BF16), (k * e).astype(BF16), _NT, preferred_element_type=F32)
        pair = (((ri ^ ci) >> (s.bit_length() - 1)) == 1) & ((ri & s) != 0)
        att = att + jnp.where(pair, a_s, 0.0)
        s //= 2

    st = s_ref[...]
    o = jnp.dot((q * jnp.exp(cum)).astype(BF16), st.astype(BF16), preferred_element_type=F32)
    o = o + jnp.dot(att.astype(BF16), vb, preferred_element_type=F32)
    o_ref[...] = _gla_out(o, ng_ref[...], r_ref[...]).astype(o_ref.dtype)

    last = cum[c - 1:c, :]
    kd_t = (k * jnp.exp(last - cum)).T.astype(BF16)
    e_col = jnp.exp(cum.T[:, c - 1:c])
    s_ref[...] = e_col * st + jnp.dot(kd_t, vb, preferred_element_type=F32)


def _gla_prompt(proj, gl, w_gate_up, b_gate, norm_g, batch, seq, t_out):
    h = GLA_HEADS
    dk = w_gate_up.shape[-1]
    dkh = dk // h
    dv = (proj.shape[1] - 2 * dk) // 2
    dvh = dv // h
    c = GLA_CHUNK
    nc = seq // c

    def rows(b, hh, i):
        return b * nc + i

    return pl.pallas_call(
        _gla_prompt_kernel,
        grid=(batch, h, nc),
        in_specs=[pl.BlockSpec((c, dkh), lambda b, hh, i: (rows(b, hh, i), hh)),
                  pl.BlockSpec((c, dkh), lambda b, hh, i: (rows(b, hh, i), h + hh)),
                  pl.BlockSpec((c, dvh), lambda b, hh, i: (rows(b, hh, i), 2 * dk // dvh + hh)),
                  pl.BlockSpec((c, dvh), lambda b, hh, i: (rows(b, hh, i), (2 * dk + dv) // dvh + hh)),
                  pl.BlockSpec((c, GLA_GATE_RANK), lambda b, hh, i: (rows(b, hh, i), 0)),
                  pl.BlockSpec((None, GLA_GATE_RANK, dkh), lambda b, hh, i: (0, 0, hh)),
                  pl.BlockSpec((1, dkh), lambda b, hh, i: (0, hh)),
                  pl.BlockSpec((1, dvh), lambda b, hh, i: (0, 0))],
        out_specs=[pl.BlockSpec((c, dvh), lambda b, hh, i: (rows(b, hh, i), hh)),
                   pl.BlockSpec((None, None, None, dkh, dvh), lambda b, hh, i: (0, b, hh, 0, 0))],
        out_shape=[jax.ShapeDtypeStruct((t_out, dv), BF16),
                   jax.ShapeDtypeStruct((1, batch, h, dkh, dvh), F32)],
        compiler_params=_params("parallel", "parallel", "arbitrary"),
    )(proj, proj, proj, proj, gl, w_gate_up, b_gate, norm_g)


def _gla_gate_kernel(gl_ref, wgu_ref, bg_ref, o_ref):
    o_ref[...] = _log_decay(gl_ref[...], wgu_ref[...], bg_ref[...])


def _gla_gate(gl, w_gate_up, b_gate, row0, rows):
    dk = w_gate_up.shape[-1]
    return pl.pallas_call(
        _gla_gate_kernel,
        grid=(1,),
        in_specs=[pl.BlockSpec((rows, GLA_GATE_RANK), lambda i: (row0 // rows, 0)),
                  pl.BlockSpec((None, GLA_GATE_RANK, dk), lambda i: (0, 0, 0)),
                  pl.BlockSpec((1, dk), lambda i: (0, 0))],
        out_specs=pl.BlockSpec((rows, dk), lambda i: (0, 0)),
        out_shape=jax.ShapeDtypeStruct((rows, dk), F32),
        compiler_params=_params("arbitrary"),
    )(gl, w_gate_up, b_gate)


def _gla_decode_kernel(qc_ref, v_ref, r_ref, kc_ref, gc_ref, ng_ref, s_ref, o_ref, ns_ref):
    dkh = qc_ref.shape[0]
    sn = jnp.exp(gc_ref[...]) * s_ref[...] + kc_ref[...] * v_ref[...]
    ns_ref[...] = sn
    o = jnp.sum((qc_ref[...] * (dkh ** -0.5)) * sn, axis=0, keepdims=True)
    o_ref[...] = _gla_out(o, ng_ref[...], r_ref[...])


def _gla_decode(proj_s, log_a, norm_g, state, layer):
    b, h, dkh, dvh = state.shape[1:]
    dk, dv = h * dkh, h * dvh
    q_col = proj_s[:, 0, :dk].reshape(b, h, dkh, 1)
    k_col = proj_s[:, 0, dk:2 * dk].reshape(b, h, dkh, 1)
    g_col = log_a.reshape(b, h, dkh, 1)
    col_spec = pl.BlockSpec((None, None, dkh, 1), lambda i, hh: (i, hh, 0, 0))
    st_spec = pl.BlockSpec((None, None, None, dkh, dvh), lambda i, hh: (layer, i, hh, 0, 0))
    st_out_spec = pl.BlockSpec((None, None, None, dkh, dvh), lambda i, hh: (0, i, hh, 0, 0))
    return pl.pallas_call(
        _gla_decode_kernel,
        grid=(b, h),
        in_specs=[col_spec,
                  pl.BlockSpec((None, 1, dvh), lambda i, hh: (i, 0, 2 * dk // dvh + hh)),
                  pl.BlockSpec((None, 1, dvh), lambda i, hh: (i, 0, (2 * dk + dv) // dvh + hh)),
                  col_spec, col_spec,
                  pl.BlockSpec((1, dvh), lambda i, hh: (0, 0)),
                  st_spec],
        out_specs=[pl.BlockSpec((None, 1, dvh), lambda i, hh: (i, 0, hh)), st_out_spec],
        out_shape=[jax.ShapeDtypeStruct((b, 1, dv), F32), jax.ShapeDtypeStruct((1, b, h, dkh, dvh), F32)],
        compiler_params=_params("parallel", "parallel"),
    )(q_col, proj_s, proj_s, k_col, g_col, norm_g, state)


def _pack_bf16_halves(y):
    half = y.shape[1] // 2
    bits = lax.bitcast_convert_type(y.astype(BF16).astype(F32), jnp.uint32)
    return (bits[:, :half] >> 16) | (bits[:, half:] & jnp.uint32(0xFFFF0000))


def _unpack_bf16_halves(u):
    lo = lax.bitcast_convert_type(u << 16, F32)
    hi = lax.bitcast_convert_type(u & jnp.uint32(0xFFFF0000), F32)
    return jnp.concatenate([lo, hi], axis=1).astype(BF16)


def _for_rows(n, fn, unroll=8):
    def group(j, c):
        for u in range(unroll):
            fn(j * unroll + u)
        return c

    def single(r, c):
        fn(r)
        return c

    full = n // unroll
    lax.fori_loop(0, full, group, 0)
    lax.fori_loop(full * unroll, n, single, 0)


def _moe_kernel(ie_ref, ir_ref, in_ref, tok_ref, dst_ref, x_hbm, wg_ref, wu_ref, wd_ref, y_hbm,
                xbuf, ybuf, gsem, ssem):
    i = pl.program_id(0)
    f = pl.program_id(1)
    ni = pl.num_programs(0)
    nf = pl.num_programs(1)
    n = in_ref[i]
    slot = i % 2
    rb = ybuf.shape[0]
    sizes = sorted({min(m, rb) for m in MOE_ROW_SIZES})

    def gather(item, sl, r):
        src = x_hbm.at[pl.ds(tok_ref[ir_ref[item] + r], 1)]
        return pltpu.make_async_copy(src, xbuf.at[sl, pl.ds(r, 1)], gsem.at[sl])

    def scatter(item, r):
        dst = y_hbm.at[pl.ds(dst_ref[ir_ref[item] + r], 1)]
        return pltpu.make_async_copy(ybuf.at[pl.ds(r, 1)], dst, ssem)

    @pl.when(f == 0)
    def _():
        @pl.when(i == 0)
        def _():
            xbuf[...] = jnp.zeros_like(xbuf)
            ybuf[...] = jnp.zeros_like(ybuf)
            _for_rows(n, lambda r: gather(0, 0, r).start())

        nxt = jnp.minimum(i + 1, ni - 1)
        n_nxt = jnp.where(i + 1 < ni, in_ref[nxt], 0)
        _for_rows(n_nxt, lambda r: gather(nxt, 1 - slot, r).start())
        _for_rows(n, lambda r: gather(i, slot, r).wait())
        prv = jnp.maximum(i - 1, 0)
        n_prv = jnp.where(i > 0, in_ref[prv], 0)
        _for_rows(n_prv, lambda r: scatter(prv, r).wait())

    lo = 0
    for m in sizes:
        @pl.when((n > lo) & (n <= m))
        def _(m=m):
            xs = _unpack_bf16_halves(xbuf[slot, :m, :])
            hg = jnp.dot(xs, wg_ref[...].astype(BF16), preferred_element_type=F32)
            hu = jnp.dot(xs, wu_ref[...].astype(BF16), preferred_element_type=F32)
            act = (hg * _sigmoid(hg) * hu).astype(BF16)
            y = jnp.dot(act, wd_ref[...].astype(BF16), preferred_element_type=F32)
            ybuf[:m, :] = y + jnp.where(f > 0, ybuf[:m, :], 0.0)
        lo = m

    @pl.when(f == nf - 1)
    def _():
        _for_rows(n, lambda r: scatter(i, r).start())

        @pl.when(i == ni - 1)
        def _():
            _for_rows(n, lambda r: scatter(i, r).wait())


def _moe_experts(xn, eid, w_gate, w_up, w_down, layer):
    t, d = xn.shape[0], xn.shape[1] * 2
    n_exp, _, dff = w_gate.shape[1:]
    a = t * MOE_TOP_K
    rb = min(MOE_ROW_BLOCK, -(-a // 8) * 8)
    fc = min(MOE_FF_TILE, dff)
    nf = dff // fc
    n_items = n_exp + a // rb

    flat = eid.reshape(-1)
    order = jnp.argsort(flat, stable=True).astype(jnp.int32)
    counts = jnp.sum((flat[:, None] == jnp.arange(n_exp, dtype=jnp.int32)[None, :]).astype(jnp.int32), axis=0)
    starts = jnp.cumsum(counts) - counts
    per_e = (counts + rb - 1) // rb
    item_end = jnp.cumsum(per_e)
    item_start = item_end - per_e
    ii = jnp.arange(n_items, dtype=jnp.int32)
    valid = ii < item_end[-1]
    e_of = jnp.minimum(jnp.searchsorted(item_end, ii, side="right").astype(jnp.int32), n_exp - 1)
    local = ii - item_start[e_of]
    item_rows = jnp.where(valid, jnp.clip(counts[e_of] - local * rb, 0, rb), 0).astype(jnp.int32)
    item_row0 = jnp.where(valid, starts[e_of] + local * rb, 0).astype(jnp.int32)
    e_last = jnp.max(jnp.where(valid, e_of, 0))
    item_e = jnp.where(valid, e_of, e_last).astype(jnp.int32)
    tok = order // MOE_TOP_K
    dst = (order % MOE_TOP_K) * t + tok

    def w_in_map(i, f, ie, ir, inn, tk, ds_):
        return (layer, ie[i], 0, jnp.where(inn[i] > 0, f, nf - 1))

    def w_out_map(i, f, ie, ir, inn, tk, ds_):
        return (layer, ie[i], jnp.where(inn[i] > 0, f, nf - 1), 0)

    grid_spec = pltpu.PrefetchScalarGridSpec(
        num_scalar_prefetch=5,
        grid=(n_items, nf),
        in_specs=[pl.BlockSpec(memory_space=pl.ANY),
                  pl.BlockSpec((None, None, d, fc), w_in_map),
                  pl.BlockSpec((None, None, d, fc), w_in_map),
                  pl.BlockSpec((None, None, fc, d), w_out_map)],
        out_specs=pl.BlockSpec(memory_space=pl.ANY),
        scratch_shapes=[pltpu.VMEM((2, rb, d // 2), jnp.uint32), pltpu.VMEM((rb, d), F32),
                        pltpu.SemaphoreType.DMA((2,)), pltpu.SemaphoreType.DMA(())],
    )
    y = pl.pallas_call(
        _moe_kernel,
        grid_spec=grid_spec,
        out_shape=jax.ShapeDtypeStruct((a, d), F32),
        compiler_params=_params("arbitrary", "arbitrary"),
    )(item_e, item_row0, item_rows, tok, dst, xn, w_gate, w_up, w_down)
    return y.reshape(MOE_TOP_K, t, d)


def _combine_norm_kernel(h_ref, y0_ref, y1_ref, gate_ref, g_ref, ho_ref, a_ref):
    h = h_ref[...] + (y0_ref[...] * gate_ref[:, 0:1] + y1_ref[...] * gate_ref[:, 1:2])
    ho_ref[...] = h
    a_ref[...] = _rms(h, g_ref[...]).astype(a_ref.dtype)


def _combine_norm(h, y2, gates, g, layer, tm=ROW_TILE // 2):
    t, d = h.shape
    return pl.pallas_call(
        _combine_norm_kernel,
        grid=_row_grid(t, tm),
        in_specs=[pl.BlockSpec((tm, d), lambda i: (i, 0)),
                  pl.BlockSpec((None, tm, d), lambda i: (0, i, 0)),
                  pl.BlockSpec((None, tm, d), lambda i: (1, i, 0)),
                  pl.BlockSpec((tm, MOE_TOP_K), lambda i: (i, 0)),
                  pl.BlockSpec((None, 1, d), lambda i: (layer, 0, 0))],
        out_specs=[pl.BlockSpec((tm, d), lambda i: (i, 0)), pl.BlockSpec((tm, d), lambda i: (i, 0))],
        out_shape=[jax.ShapeDtypeStruct((t, d), F32), jax.ShapeDtypeStruct((t, d), BF16)],
        compiler_params=_params("parallel"),
    )(h, y2, y2, gates, g)


def kernel(x_prompt, x_sample, cache_swa_k, cache_swa_v, state_gla, p_prompt, p_sample, ln_mix, ln_ffn, ln_ple, ln_final, swa_w_in, swa_sinks, swa_w_out, gla_w_in, gla_w_gate_up, gla_b_gate, gla_norm, gla_w_out, moe_w_group, moe_b_group, moe_w_expert, moe_b_expert, moe_w_gate, moe_w_up, moe_w_down, ple_w_proj, ple_w_gate):
    batch, seq, d = x_prompt.shape
    dec = x_sample.shape[0]
    assert x_sample.shape[1] == 1
    tp = batch * seq
    t = tp + dec
    depth = ln_mix.shape[0]
    kvh = cache_swa_k.shape[3]
    kvd = kvh * SWA_HEAD_DIM
    dk = gla_w_gate_up.shape[-1]
    dv = (gla_w_in.shape[-1] - GLA_GATE_RANK - 2 * dk) // 2

    x_p, x_s = x_prompt.reshape(tp, d), x_sample.reshape(dec, d)
    h = None
    p_all = jnp.concatenate([p_prompt.reshape(depth, tp, -1), p_sample.reshape(depth, dec, -1)], axis=1)
    pad = ROUTER_LANES - MOE_GROUPS - MOE_EXPERTS
    w_router = jnp.concatenate([moe_w_group, moe_w_expert, jnp.zeros((depth, d, pad), F32)], axis=-1)
    b_router = jnp.concatenate([moe_b_group, moe_b_expert, jnp.zeros((depth, pad), F32)], axis=-1)
    ln_mix, ln_ffn, ln_ple = (g.reshape(depth, 1, d) for g in (ln_mix, ln_ffn, ln_ple))

    new_k_p, new_v_p, new_s_p, new_k_s, new_v_s, new_s_s = [], [], [], [], [], []
    for i in range(depth):
        j = i // 2
        if i % 2 == 0:
            if h is None:
                qkv = _matmul(_rmsnorm(x_p, ln_mix, i), swa_w_in, j, swa_w_in.shape[-1],
                              a_tail=_rmsnorm(x_s, ln_mix, i))
            else:
                qkv = _matmul(_rmsnorm(h, ln_mix, i), swa_w_in, j, swa_w_in.shape[-1])
            o, nk_p, nv_p = _swa_prompt(qkv, swa_sinks[j], batch, seq, kvh)
            o_s, nk, nv = _swa_decode(qkv, swa_sinks[j], cache_swa_k, cache_swa_v, j, tp)
            new_k_p.append(nk_p.reshape(1, batch, WINDOW, kvh, SWA_HEAD_DIM))
            new_v_p.append(nv_p.reshape(1, batch, WINDOW, kvh, SWA_HEAD_DIM))
            new_k_s.append(nk)
            new_v_s.append(nv)
            if h is None:
                h = _matmul(o, swa_w_out, j, d, mode="res", res=x_p, res_tail=x_s, a_tail=o_s.astype(BF16))
            else:
                h = _matmul(o, swa_w_out, j, d, mode="res", res=h, a_tail=o_s.astype(BF16))
        else:
            w_gl = lax.slice(gla_w_in, (j, 0, 2 * dk + 2 * dv), (j + 1, d, gla_w_in.shape[-1]))[0]
            a, gl = _rmsnorm_proj(h, ln_mix, i, w_gl)
            proj = _matmul(a, gla_w_in, j, 2 * dk + 2 * dv)
            o, s_p = _gla_prompt(proj, gl, gla_w_gate_up[j:j + 1], gla_b_gate[j:j + 1], gla_norm[j:j + 1], batch, seq, tp)
            log_a = _gla_gate(gl, gla_w_gate_up[j:j + 1], gla_b_gate[j:j + 1], tp, dec)
            o_s, s_s = _gla_decode(proj[tp:].reshape(dec, 1, -1), log_a, gla_norm[j:j + 1], state_gla, j)
            new_s_p.append(s_p)
            new_s_s.append(s_s)
            h = _matmul(o, gla_w_out, j, d, mode="res", res=h, a_tail=o_s.reshape(dec, dv).astype(BF16))
        xn, eid, gates = _rmsnorm_router(h, ln_ffn, i, w_router[i], b_router[i:i + 1])
        y2 = _moe_experts(xn, eid, moe_w_gate, moe_w_up, moe_w_down, i)
        h, a3 = _combine_norm(h, y2, gates, ln_ple, i)
        h = _matmul(a3, ple_w_gate, i, d, mode="ple", res=h, p=p_all, wp=ple_w_proj)

    g_fin = ln_final.reshape(1, 1, d)
    y_p = _rmsnorm(h, g_fin, 0, out_dtype=F32, rows=tp)
    y_s = _rmsnorm(h, g_fin, 0, out_dtype=F32, rows=dec, row_block0=tp // dec, tm=dec)
    return (y_p.reshape(batch, seq, d), y_s.reshape(dec, 1, d),
            *(jnp.concatenate(per_layer) for per_layer in (new_k_p, new_v_p, new_s_p, new_k_s, new_v_s, new_s_s)))
```

```python
import functools

import jax
import jax.numpy as jnp
from jax import lax
from jax.experimental import pallas as pl
from jax.experimental.pallas import tpu as pltpu

F32 = jnp.float32
BF16 = jnp.bfloat16

EPS = 1e-6
WINDOW = 128
SWA_HEAD_DIM = 128
GLA_HEADS = 4
GLA_GATE_RANK = 16
GLA_TAU = 16.0
GLA_CHUNK = 256
MOE_GROUPS = 8
MOE_EXPERTS_PER_GROUP = 8
MOE_EXPERTS = MOE_GROUPS * MOE_EXPERTS_PER_GROUP
MOE_TOP_K = 2
ROUTER_LANES = 128

V7X_VMEM_LIMIT_BYTES = 60 * 1024 * 1024
ROW_TILE = 512
MM_ROW_TILE = 1024
COL_TILE = 512
MOE_ROW_BLOCK = 512
MOE_ROW_SIZES = (256, 384, 512)
MOE_FF_TILE = 256
CAST_ROWS = 256


def _params(*sem):
    return pltpu.CompilerParams(dimension_semantics=sem, vmem_limit_bytes=V7X_VMEM_LIMIT_BYTES)


def _sigmoid(x):
    return 1.0 / (1.0 + jnp.exp(-x))


def _cast_rows(src_ref, dst_ref):
    rows = src_ref.shape[0]
    step = min(CAST_ROWS, rows)

    def body(i, c):
        r = pl.multiple_of(i * step, step)
        dst_ref[pl.ds(r, step), :] = src_ref[pl.ds(r, step), :].astype(BF16)
        return c

    lax.fori_loop(0, rows // step, body, 0)


def _rms(x, g):
    ms = jnp.mean(x * x, axis=-1, keepdims=True)
    return x * lax.rsqrt(ms + EPS) * g


def _norm_kernel(h_ref, g_ref, o_ref):
    o_ref[...] = _rms(h_ref[...], g_ref[...]).astype(o_ref.dtype)


def _norm_proj_kernel(h_ref, g_ref, w_ref, o_ref, s_ref):
    y = _rms(h_ref[...], g_ref[...]).astype(BF16)
    o_ref[...] = y
    s_ref[...] = jnp.dot(y, w_ref[...].astype(BF16), preferred_element_type=F32)


def _route(logits):
    lane = lax.broadcasted_iota(jnp.int32, logits.shape, 1)
    neg = jnp.float32(-jnp.inf)
    is_g = lane < MOE_GROUPS
    lg = jnp.where(is_g, logits, neg)
    mg = jnp.max(lg, axis=-1, keepdims=True)
    g_sel = jnp.min(jnp.where(lg == mg, lane, ROUTER_LANES), axis=-1, keepdims=True)
    p_g = 1.0 / jnp.sum(jnp.where(is_g, jnp.exp(lg - mg), 0.0), axis=-1, keepdims=True)
    lo = MOE_GROUPS + g_sel * MOE_EXPERTS_PER_GROUP
    in_grp = (lane >= lo) & (lane < lo + MOE_EXPERTS_PER_GROUP)
    le = jnp.where(in_grp, logits, neg)
    v0 = jnp.max(le, axis=-1, keepdims=True)
    i0 = jnp.min(jnp.where(le == v0, lane, ROUTER_LANES), axis=-1, keepdims=True)
    le1 = jnp.where(lane == i0, neg, le)
    v1 = jnp.max(le1, axis=-1, keepdims=True)
    i1 = jnp.min(jnp.where(le1 == v1, lane, ROUTER_LANES), axis=-1, keepdims=True)
    e1 = jnp.exp(v1 - v0)
    g0 = p_g / (1.0 + e1)
    g1 = p_g * e1 / (1.0 + e1)
    return i0 - MOE_GROUPS, i1 - MOE_GROUPS, g0, g1


def _norm_router_kernel(h_ref, g_ref, w_ref, b_ref, o_ref, eid_ref, gate_ref, logit_ref, *, fine_tiles):
    y = _rms(h_ref[...], g_ref[...])
    o_ref[...] = _pack_bf16_halves(y)
    w = w_ref[...]
    yh = y.astype(BF16)
    wh = w.astype(BF16)
    rounded = jnp.dot(yh, wh, preferred_element_type=F32)
    logit_ref[...] = rounded

    @pl.when(pl.program_id(0) < fine_tiles)
    def _():
        yl = (y - yh.astype(F32)).astype(BF16)
        wl = (w - wh.astype(F32)).astype(BF16)
        fine = rounded + (jnp.dot(yh, wl, preferred_element_type=F32) + jnp.dot(yl, wh, preferred_element_type=F32))
        lane = lax.broadcasted_iota(jnp.int32, fine.shape, 1)
        logit_ref[...] = jnp.where(lane < MOE_GROUPS, rounded, fine)

    e0, e1, g0, g1 = _route(logit_ref[...] + b_ref[...])
    eid_ref[:, 0:1] = e0
    eid_ref[:, 1:2] = e1
    gate_ref[:, 0:1] = g0
    gate_ref[:, 1:2] = g1


def _row_grid(t, tm):
    return (pl.cdiv(t, tm),)


def _rmsnorm(h, g, layer, out_dtype=BF16, rows=None, row_block0=0, tm=ROW_TILE):
    t, d = h.shape
    rows = t if rows is None else rows
    tm = min(tm, rows)
    return pl.pallas_call(
        _norm_kernel,
        grid=_row_grid(rows, tm),
        in_specs=[pl.BlockSpec((tm, d), lambda i: (row_block0 + i, 0)),
                  pl.BlockSpec((None, 1, d), lambda i: (layer, 0, 0))],
        out_specs=pl.BlockSpec((tm, d), lambda i: (i, 0)),
        out_shape=jax.ShapeDtypeStruct((rows, d), out_dtype),
        compiler_params=_params("parallel"),
    )(h, g)


def _rmsnorm_proj(h, g, layer, w_small, tm=ROW_TILE):
    t, d = h.shape
    ns = w_small.shape[1]
    return pl.pallas_call(
        _norm_proj_kernel,
        grid=_row_grid(t, tm),
        in_specs=[pl.BlockSpec((tm, d), lambda i: (i, 0)),
                  pl.BlockSpec((None, 1, d), lambda i: (layer, 0, 0)),
                  pl.BlockSpec((d, ns), lambda i: (0, 0))],
        out_specs=[pl.BlockSpec((tm, d), lambda i: (i, 0)),
                   pl.BlockSpec((tm, ns), lambda i: (i, 0))],
        out_shape=[jax.ShapeDtypeStruct((t, d), BF16), jax.ShapeDtypeStruct((t, ns), F32)],
        compiler_params=_params("parallel"),
    )(h, g, w_small)


def _rmsnorm_router(h, g, layer, w_router, b_router, prompt_rows, tm=ROW_TILE // 2):
    t, d = h.shape
    assert prompt_rows % tm == 0
    return pl.pallas_call(
        functools.partial(_norm_router_kernel, fine_tiles=prompt_rows // tm),
        grid=_row_grid(t, tm),
        in_specs=[pl.BlockSpec((tm, d), lambda i: (i, 0)),
                  pl.BlockSpec((None, 1, d), lambda i: (layer, 0, 0)),
                  pl.BlockSpec((d, ROUTER_LANES), lambda i: (0, 0)),
                  pl.BlockSpec((1, ROUTER_LANES), lambda i: (0, 0))],
        out_specs=[pl.BlockSpec((tm, d // 2), lambda i: (i, 0)),
                   pl.BlockSpec((tm, MOE_TOP_K), lambda i: (i, 0)),
                   pl.BlockSpec((tm, MOE_TOP_K), lambda i: (i, 0))],
        out_shape=[jax.ShapeDtypeStruct((t, d // 2), jnp.uint32),
                   jax.ShapeDtypeStruct((t, MOE_TOP_K), jnp.int32),
                   jax.ShapeDtypeStruct((t, MOE_TOP_K), F32)],
        scratch_shapes=[pltpu.VMEM((tm, ROUTER_LANES), F32)],
        compiler_params=_params("parallel"),
    )(h, g, w_router, b_router)


def _mm_kernel(*refs, mode, tail, split_a, split_r):
    a_ref, refs = refs[0], refs[1:]
    at_ref = a_ref
    if split_a:
        at_ref, refs = refs[0], refs[1:]
    w_ref, refs = refs[0], refs[1:]
    r_ref = rt_ref = None
    if mode != "plain":
        r_ref, refs = refs[0], refs[1:]
        rt_ref = r_ref
        if split_r:
            rt_ref, refs = refs[0], refs[1:]
    if mode == "ple":
        p_ref, wp_ref, o_ref, wb_ref, wpb_ref = refs
    else:
        o_ref, wb_ref = refs
    m = pl.program_id(1)

    @pl.when(m == 0)
    def _():
        _cast_rows(w_ref, wb_ref)
        if mode == "ple":
            _cast_rows(wp_ref, wpb_ref)

    def tile(src_ref, res_ref, rows):
        acc = jnp.dot(src_ref[rows, :], wb_ref[...], preferred_element_type=F32)
        if mode == "plain":
            o_ref[rows, :] = acc.astype(o_ref.dtype)
        elif mode == "res":
            o_ref[rows, :] = res_ref[rows, :] + acc
        else:
            pp = jnp.dot(p_ref[rows, :].astype(BF16), wpb_ref[...], preferred_element_type=F32)
            o_ref[rows, :] = res_ref[rows, :] + pp * _sigmoid(acc)

    if tail == 0:
        tile(a_ref, r_ref, slice(None))
    else:
        last = pl.num_programs(1) - 1
        pl.when(m < last)(lambda: tile(a_ref, r_ref, slice(None)))
        pl.when(m == last)(lambda: tile(at_ref, rt_ref, slice(0, tail)))


def _matmul(a, w, layer, n_cols, mode="plain", res=None, p=None, wp=None, a_tail=None, res_tail=None,
            out_dtype=F32, tm=MM_ROW_TILE, tn=COL_TILE):
    k = a.shape[1]
    tm = min(tm, a.shape[0] // 256 * 256)
    split_a = a_tail is not None
    split_r = res_tail is not None
    if split_a:
        assert a.shape[0] % tm == 0 and a_tail.shape[0] < tm
        t, tail = a.shape[0] + a_tail.shape[0], a_tail.shape[0]
    else:
        t, tail = a.shape[0], a.shape[0] % tm
    if split_r:
        assert res.shape[0] == t - tail and res_tail.shape[0] == tail
    n_full = t // tm

    def full_rows(n, m):
        return jnp.minimum(m, n_full - 1)

    tn = min(tn, n_cols)
    assert n_cols % tn == 0
    grid = (n_cols // tn, pl.cdiv(t, tm))
    if split_a:
        in_specs = [pl.BlockSpec((tm, k), lambda n, m: (full_rows(n, m), 0)),
                    pl.BlockSpec((tail, k), lambda n, m: (0, 0))]
        args = [a, a_tail]
    else:
        in_specs = [pl.BlockSpec((tm, k), lambda n, m: (m, 0))]
        args = [a]
    in_specs.append(pl.BlockSpec((None, k, tn), lambda n, m: (layer, 0, n)))
    args.append(w)
    scratch = [pltpu.VMEM((k, tn), BF16)]
    if mode in ("res", "ple"):
        if split_r:
            in_specs += [pl.BlockSpec((tm, tn), lambda n, m: (full_rows(n, m), n)),
                         pl.BlockSpec((tail, tn), lambda n, m: (0, n))]
            args += [res, res_tail]
        else:
            in_specs.append(pl.BlockSpec((tm, tn), lambda n, m: (m, n)))
            args.append(res)
    if mode == "ple":
        kp = p.shape[-1]
        in_specs += [pl.BlockSpec((None, tm, kp), lambda n, m: (layer, m, 0)),
                     pl.BlockSpec((None, kp, tn), lambda n, m: (layer, 0, n))]
        args += [p, wp]
        scratch.append(pltpu.VMEM((kp, tn), BF16))
    return pl.pallas_call(
        functools.partial(_mm_kernel, mode=mode, tail=tail, split_a=split_a, split_r=split_r),
        grid=grid,
        in_specs=in_specs,
        out_specs=pl.BlockSpec((tm, tn), lambda n, m: (m, n)),
        out_shape=jax.ShapeDtypeStruct((t, n_cols), out_dtype),
        scratch_shapes=scratch,
        compiler_params=_params("arbitrary", "arbitrary"),
    )(*args)


def _softmax_sink_pv(s, sk, vb):
    m = jnp.maximum(jnp.max(s, axis=-1, keepdims=True), sk)
    p = jnp.exp(s - m)
    den = jnp.sum(p, axis=-1, keepdims=True) + jnp.exp(sk - m)
    return jnp.dot((p / den).astype(BF16), vb, preferred_element_type=F32)


_NT = (((1,), (1,)), ((), ()))


def _swa_prompt_kernel(sinks_ref, q_ref, kp_ref, kc_ref, vp_ref, vc_ref, o_ref, nk_ref, nv_ref, *, kvh, group):
    n = pl.program_id(1)
    w, kvd = kc_ref.shape
    hd = kvd // kvh
    kk = jnp.concatenate([kp_ref[...], kc_ref[...]], axis=0).astype(BF16)
    vv = jnp.concatenate([vp_ref[...], vc_ref[...]], axis=0).astype(BF16)
    qi = lax.broadcasted_iota(jnp.int32, (w, 2 * w), 0)
    kj = lax.broadcasted_iota(jnp.int32, (w, 2 * w), 1)
    mask = (kj > qi) & (kj <= qi + w) & (kj >= jnp.where(n > 0, 0, w))
    scale = hd ** -0.5
    for kv in range(kvh):
        kh = kk[:, kv * hd:(kv + 1) * hd]
        vh = vv[:, kv * hd:(kv + 1) * hd]
        for g in range(group):
            c0 = (kv * group + g) * hd
            qg = (q_ref[:, c0:c0 + hd] * scale).astype(BF16)
            s = lax.dot_general(qg, kh, _NT, preferred_element_type=F32)
            s = jnp.where(mask, s, -jnp.inf)
            o = _softmax_sink_pv(s, sinks_ref[kv * group + g], vh)
            o_ref[:, c0:c0 + hd] = o.astype(o_ref.dtype)

    @pl.when(n == pl.num_programs(1) - 1)
    def _():
        nk_ref[...] = kc_ref[...]
        nv_ref[...] = vc_ref[...]


def _swa_prompt(qkv, sinks, batch, seq, kvh):
    hd, w = SWA_HEAD_DIM, WINDOW
    kvd = kvh * hd
    qd = qkv.shape[1] - 2 * kvd
    group = qd // kvd
    nb = seq // w
    kcol = qd // kvd

    def cur(b, n):
        return b * nb + n

    def prev(b, n):
        return b * nb + jnp.maximum(n - 1, 0)

    last_spec = pl.BlockSpec((None, None, w, kvd), lambda b, n: (0, b, 0, 0))
    return pl.pallas_call(
        functools.partial(_swa_prompt_kernel, kvh=kvh, group=group),
        grid=(batch, nb),
        in_specs=[pl.BlockSpec(memory_space=pltpu.SMEM),
                  pl.BlockSpec((w, qd), lambda b, n: (cur(b, n), 0)),
                  pl.BlockSpec((w, kvd), lambda b, n: (prev(b, n), kcol)),
                  pl.BlockSpec((w, kvd), lambda b, n: (cur(b, n), kcol)),
                  pl.BlockSpec((w, kvd), lambda b, n: (prev(b, n), kcol + 1)),
                  pl.BlockSpec((w, kvd), lambda b, n: (cur(b, n), kcol + 1))],
        out_specs=[pl.BlockSpec((w, qd), lambda b, n: (cur(b, n), 0)), last_spec, last_spec],
        out_shape=[jax.ShapeDtypeStruct((batch * seq, qd), BF16),
                   jax.ShapeDtypeStruct((1, batch, w, kvd), F32),
                   jax.ShapeDtypeStruct((1, batch, w, kvd), F32)],
        compiler_params=_params("parallel", "arbitrary"),
    )(sinks, qkv, qkv, qkv, qkv, qkv)


def _swa_decode_kernel(sk_ref, q_ref, x_ref, kp_ref, vp_ref, nk_ref, nv_ref, o_ref, *, kvh, group):
    bb, w, kvd = kp_ref.shape
    hd = kvd // kvh
    nh = kvh * group
    qd = nh * hd
    scale = hd ** -0.5
    row = lax.broadcasted_iota(jnp.int32, (w, kvd), 0)
    head_kv = lax.broadcasted_iota(jnp.int32, (nh, hd), 0) // group
    sk = sk_ref[...]
    for bi in range(bb):
        x = x_ref[bi:bi + 1, :]
        kc = jnp.where(row == w - 1, x[:, qd:qd + kvd], pltpu.roll(kp_ref[bi], w - 1, axis=0))
        vc = jnp.where(row == w - 1, x[:, qd + kvd:], pltpu.roll(vp_ref[bi], w - 1, axis=0))
        nk_ref[bi] = kc
        nv_ref[bi] = vc
        s = lax.dot_general((q_ref[bi] * scale).astype(BF16), kc.astype(BF16), _NT, preferred_element_type=F32)
        ov = _softmax_sink_pv(s, sk, vc.astype(BF16))
        o = jnp.zeros((nh, hd), F32)
        for kv in range(kvh):
            o = jnp.where(head_kv == kv, ov[:, kv * hd:(kv + 1) * hd], o)
        o_ref[bi] = o


def _swa_decode(qkv, sinks, k_past, v_past, layer, row0, bb=8):
    nl, b, w, kvh, hd = k_past.shape
    kvd = kvh * hd
    qd = qkv.shape[1] - 2 * kvd
    group = qd // kvd
    nh = kvh * group
    bb = min(bb, b)
    blk0 = row0 // bb
    q = qkv[row0:row0 + b, :qd].reshape(b, nh, 1, hd)
    own = (jnp.arange(nh)[:, None] // group == jnp.arange(kvh)[None, :])[None, :, :, None]
    q_blocks = jnp.where(own, q, 0.0).reshape(b, nh, kvd)
    in_spec = pl.BlockSpec((None, bb, w, kvd), lambda i: (layer, i, 0, 0))
    out_spec = pl.BlockSpec((None, bb, w, kvd), lambda i: (0, i, 0, 0))
    nk, nv, o = pl.pallas_call(
        functools.partial(_swa_decode_kernel, kvh=kvh, group=group),
        grid=(b // bb,),
        in_specs=[pl.BlockSpec((nh, 1), lambda i: (0, 0)),
                  pl.BlockSpec((bb, nh, kvd), lambda i: (i, 0, 0)),
                  pl.BlockSpec((bb, qkv.shape[1]), lambda i: (blk0 + i, 0)),
                  in_spec, in_spec],
        out_specs=[out_spec, out_spec, pl.BlockSpec((bb, nh, hd), lambda i: (i, 0, 0))],
        out_shape=[jax.ShapeDtypeStruct((1, b, w, kvd), F32), jax.ShapeDtypeStruct((1, b, w, kvd), F32),
                   jax.ShapeDtypeStruct((b, nh, hd), F32)],
        compiler_params=_params("parallel"),
    )(sinks.reshape(nh, 1), q_blocks, qkv, k_past.reshape(nl, b, w, kvd), v_past.reshape(nl, b, w, kvd))
    return o.reshape(b, qd), nk.reshape(1, b, w, kvh, hd), nv.reshape(1, b, w, kvh, hd)


def _log_decay(gl, wgu, bg):
    x = jnp.dot(gl.astype(BF16), wgu.astype(BF16), preferred_element_type=F32) + bg
    return (jnp.minimum(x, 0.0) - jnp.log(1.0 + jnp.exp(-jnp.abs(x)))) * (1.0 / GLA_TAU)


def _gla_out(o, ng, r):
    return _rms(o, ng) * (r * _sigmoid(r))


def _block_ref_row(x, row, s):
    c, d = x.shape
    if 2 * s >= 8:
        nblk = c // (2 * s)
        ref = x.reshape(nblk, 2 * s, d)[:, s - 1:s, :]
        return jnp.broadcast_to(ref, (nblk, 2 * s, d)).reshape(c, d)
    pos = row & (2 * s - 1)
    out = x
    for delta in range(s - 1, -s - 1, -1):
        if delta != 0:
            out = jnp.where(pos == s - 1 - delta, pltpu.roll(x, (c - delta) % c, axis=0), out)
    return out


def _gla_prompt_kernel(q_ref, k_ref, v_ref, r_ref, gl_ref, wgu_ref, bg_ref, ng_ref, o_ref, s_ref):
    c, dkh = q_ref.shape

    @pl.when(pl.program_id(2) == 0)
    def _():
        s_ref[...] = jnp.zeros_like(s_ref)

    g = _log_decay(gl_ref[...], wgu_ref[...], bg_ref[...])
    row = lax.broadcasted_iota(jnp.int32, (c, dkh), 0)
    cum = g
    sh = 1
    while sh < c:
        cum = cum + jnp.where(row >= sh, pltpu.roll(cum, sh, axis=0), 0.0)
        sh *= 2
    q = q_ref[...] * (dkh ** -0.5)
    k = k_ref[...]
    vb = v_ref[...].astype(BF16)

    ri = lax.broadcasted_iota(jnp.int32, (c, c), 0)
    ci = lax.broadcasted_iota(jnp.int32, (c, c), 1)
    att = jnp.where(ri == ci, lax.dot_general(q.astype(BF16), k.astype(BF16), _NT, preferred_element_type=F32), 0.0)
    s = c // 2
    while s >= 1:
        cref = _block_ref_row(cum, row, s)
        e = jnp.exp(jnp.where((row & s) != 0, cum - cref, cref - cum))
        a_s = lax.dot_general((q * e).astype(BF16), (k * e).astype(BF16), _NT, preferred_element_type=F32)
        pair = (((ri ^ ci) >> (s.bit_length() - 1)) == 1) & ((ri & s) != 0)
        att = att + jnp.where(pair, a_s, 0.0)
        s //= 2

    st = s_ref[...]
    o = jnp.dot((q * jnp.exp(cum)).astype(BF16), st.astype(BF16), preferred_element_type=F32)
    o = o + jnp.dot(att.astype(BF16), vb, preferred_element_type=F32)
    o_ref[...] = _gla_out(o, ng_ref[...], r_ref[...]).astype(o_ref.dtype)

    last = cum[c - 1:c, :]
    kd_t = (k * jnp.exp(last - cum)).T.astype(BF16)
    e_col = jnp.exp(cum.T[:, c - 1:c])
    s_ref[...] = e_col * st + jnp.dot(kd_t, vb, preferred_element_type=F32)


def _gla_prompt(proj, gl, w_gate_up, b_gate, norm_g, batch, seq, t_out):
    h = GLA_HEADS
    dk = w_gate_up.shape[-1]
    dkh = dk // h
    dv = (proj.shape[1] - 2 * dk) // 2
    dvh = dv // h
    c = GLA_CHUNK
    nc = seq // c

    def rows(b, hh, i):
        return b * nc + i

    return pl.pallas_call(
        _gla_prompt_kernel,
        grid=(batch, h, nc),
        in_specs=[pl.BlockSpec((c, dkh), lambda b, hh, i: (rows(b, hh, i), hh)),
                  pl.BlockSpec((c, dkh), lambda b, hh, i: (rows(b, hh, i), h + hh)),
                  pl.BlockSpec((c, dvh), lambda b, hh, i: (rows(b, hh, i), 2 * dk // dvh + hh)),
                  pl.BlockSpec((c, dvh), lambda b, hh, i: (rows(b, hh, i), (2 * dk + dv) // dvh + hh)),
                  pl.BlockSpec((c, GLA_GATE_RANK), lambda b, hh, i: (rows(b, hh, i), 0)),
                  pl.BlockSpec((None, GLA_GATE_RANK, dkh), lambda b, hh, i: (0, 0, hh)),
                  pl.BlockSpec((1, dkh), lambda b, hh, i: (0, hh)),
                  pl.BlockSpec((1, dvh), lambda b, hh, i: (0, 0))],
        out_specs=[pl.BlockSpec((c, dvh), lambda b, hh, i: (rows(b, hh, i), hh)),
                   pl.BlockSpec((None, None, None, dkh, dvh), lambda b, hh, i: (0, b, hh, 0, 0))],
        out_shape=[jax.ShapeDtypeStruct((t_out, dv), BF16),
                   jax.ShapeDtypeStruct((1, batch, h, dkh, dvh), F32)],
        compiler_params=_params("parallel", "parallel", "arbitrary"),
    )(proj, proj, proj, proj, gl, w_gate_up, b_gate, norm_g)


def _gla_gate_kernel(gl_ref, wgu_ref, bg_ref, o_ref):
    o_ref[...] = _log_decay(gl_ref[...], wgu_ref[...], bg_ref[...])


def _gla_gate(gl, w_gate_up, b_gate, row0, rows):
    dk = w_gate_up.shape[-1]
    return pl.pallas_call(
        _gla_gate_kernel,
        grid=(1,),
        in_specs=[pl.BlockSpec((rows, GLA_GATE_RANK), lambda i: (row0 // rows, 0)),
                  pl.BlockSpec((None, GLA_GATE_RANK, dk), lambda i: (0, 0, 0)),
                  pl.BlockSpec((1, dk), lambda i: (0, 0))],
        out_specs=pl.BlockSpec((rows, dk), lambda i: (0, 0)),
        out_shape=jax.ShapeDtypeStruct((rows, dk), F32),
        compiler_params=_params("arbitrary"),
    )(gl, w_gate_up, b_gate)


def _gla_decode_kernel(qc_ref, v_ref, r_ref, kc_ref, gc_ref, ng_ref, s_ref, o_ref, ns_ref):
    dkh = qc_ref.shape[0]
    sn = jnp.exp(gc_ref[...]) * s_ref[...] + kc_ref[...] * v_ref[...]
    ns_ref[...] = sn
    o = jnp.sum((qc_ref[...] * (dkh ** -0.5)) * sn, axis=0, keepdims=True)
    o_ref[...] = _gla_out(o, ng_ref[...], r_ref[...])


def _gla_decode(proj_s, log_a, norm_g, state, layer):
    b, h, dkh, dvh = state.shape[1:]
    dk, dv = h * dkh, h * dvh
    q_col = proj_s[:, 0, :dk].reshape(b, h, dkh, 1)
    k_col = proj_s[:, 0, dk:2 * dk].reshape(b, h, dkh, 1)
    g_col = log_a.reshape(b, h, dkh, 1)
    col_spec = pl.BlockSpec((None, None, dkh, 1), lambda i, hh: (i, hh, 0, 0))
    st_spec = pl.BlockSpec((None, None, None, dkh, dvh), lambda i, hh: (layer, i, hh, 0, 0))
    st_out_spec = pl.BlockSpec((None, None, None, dkh, dvh), lambda i, hh: (0, i, hh, 0, 0))
    return pl.pallas_call(
        _gla_decode_kernel,
        grid=(b, h),
        in_specs=[col_spec,
                  pl.BlockSpec((None, 1, dvh), lambda i, hh: (i, 0, 2 * dk // dvh + hh)),
                  pl.BlockSpec((None, 1, dvh), lambda i, hh: (i, 0, (2 * dk + dv) // dvh + hh)),
                  col_spec, col_spec,
                  pl.BlockSpec((1, dvh), lambda i, hh: (0, 0)),
                  st_spec],
        out_specs=[pl.BlockSpec((None, 1, dvh), lambda i, hh: (i, 0, hh)), st_out_spec],
        out_shape=[jax.ShapeDtypeStruct((b, 1, dv), F32), jax.ShapeDtypeStruct((1, b, h, dkh, dvh), F32)],
        compiler_params=_params("parallel", "parallel"),
    )(q_col, proj_s, proj_s, k_col, g_col, norm_g, state)


def _pack_bf16_halves(y):
    half = y.shape[1] // 2
    bits = lax.bitcast_convert_type(y.astype(BF16).astype(F32), jnp.uint32)
    return (bits[:, :half] >> 16) | (bits[:, half:] & jnp.uint32(0xFFFF0000))


def _unpack_bf16_halves(u):
    lo = lax.bitcast_convert_type(u << 16, F32)
    hi = lax.bitcast_convert_type(u & jnp.uint32(0xFFFF0000), F32)
    return jnp.concatenate([lo, hi], axis=1).astype(BF16)


def _for_rows(n, fn, unroll=8):
    def group(j, c):
        for u in range(unroll):
            fn(j * unroll + u)
        return c

    def single(r, c):
        fn(r)
        return c

    full = n // unroll
    lax.fori_loop(0, full, group, 0)
    lax.fori_loop(full * unroll, n, single, 0)


def _moe_kernel(ie_ref, ir_ref, in_ref, tok_ref, dst_ref, x_hbm, wg_ref, wu_ref, wd_ref, y_hbm,
                xbuf, ybuf, gsem, ssem):
    i = pl.program_id(0)
    f = pl.program_id(1)
    ni = pl.num_programs(0)
    nf = pl.num_programs(1)
    n = in_ref[i]
    slot = i % 2
    rb = ybuf.shape[0]
    sizes = sorted({min(m, rb) for m in MOE_ROW_SIZES})

    def gather(item, sl, r):
        src = x_hbm.at[pl.ds(tok_ref[ir_ref[item] + r], 1)]
        return pltpu.make_async_copy(src, xbuf.at[sl, pl.ds(r, 1)], gsem.at[sl])

    def scatter(item, r):
        dst = y_hbm.at[pl.ds(dst_ref[ir_ref[item] + r], 1)]
        return pltpu.make_async_copy(ybuf.at[pl.ds(r, 1)], dst, ssem)

    @pl.when(f == 0)
    def _():
        @pl.when(i == 0)
        def _():
            xbuf[...] = jnp.zeros_like(xbuf)
            ybuf[...] = jnp.zeros_like(ybuf)
            _for_rows(n, lambda r: gather(0, 0, r).start())

        nxt = jnp.minimum(i + 1, ni - 1)
        n_nxt = jnp.where(i + 1 < ni, in_ref[nxt], 0)
        _for_rows(n_nxt, lambda r: gather(nxt, 1 - slot, r).start())
        _for_rows(n, lambda r: gather(i, slot, r).wait())
        prv = jnp.maximum(i - 1, 0)
        n_prv = jnp.where(i > 0, in_ref[prv], 0)
        _for_rows(n_prv, lambda r: scatter(prv, r).wait())

    lo = 0
    for m in sizes:
        @pl.when((n > lo) & (n <= m))
        def _(m=m):
            xs = _unpack_bf16_halves(xbuf[slot, :m, :])
            hg = jnp.dot(xs, wg_ref[...].astype(BF16), preferred_element_type=F32)
            hu = jnp.dot(xs, wu_ref[...].astype(BF16), preferred_element_type=F32)
            act = (hg * _sigmoid(hg) * hu).astype(BF16)
            y = jnp.dot(act, wd_ref[...].astype(BF16), preferred_element_type=F32)
            ybuf[:m, :] = y + jnp.where(f > 0, ybuf[:m, :], 0.0)
        lo = m

    @pl.when(f == nf - 1)
    def _():
        _for_rows(n, lambda r: scatter(i, r).start())

        @pl.when(i == ni - 1)
        def _():
            _for_rows(n, lambda r: scatter(i, r).wait())


def _moe_experts(xn, eid, w_gate, w_up, w_down, layer):
    t, d = xn.shape[0], xn.shape[1] * 2
    n_exp, _, dff = w_gate.shape[1:]
    a = t * MOE_TOP_K
    rb = min(MOE_ROW_BLOCK, -(-a // 8) * 8)
    fc = min(MOE_FF_TILE, dff)
    nf = dff // fc
    n_items = n_exp + a // rb

    flat = eid.reshape(-1)
    order = jnp.argsort(flat, stable=True).astype(jnp.int32)
    counts = jnp.sum((flat[:, None] == jnp.arange(n_exp, dtype=jnp.int32)[None, :]).astype(jnp.int32), axis=0)
    starts = jnp.cumsum(counts) - counts
    per_e = (counts + rb - 1) // rb
    item_end = jnp.cumsum(per_e)
    item_start = item_end - per_e
    ii = jnp.arange(n_items, dtype=jnp.int32)
    valid = ii < item_end[-1]
    e_of = jnp.minimum(jnp.searchsorted(item_end, ii, side="right").astype(jnp.int32), n_exp - 1)
    local = ii - item_start[e_of]
    item_rows = jnp.where(valid, jnp.clip(counts[e_of] - local * rb, 0, rb), 0).astype(jnp.int32)
    item_row0 = jnp.where(valid, starts[e_of] + local * rb, 0).astype(jnp.int32)
    e_last = jnp.max(jnp.where(valid, e_of, 0))
    item_e = jnp.where(valid, e_of, e_last).astype(jnp.int32)
    tok = order // MOE_TOP_K
    dst = (order % MOE_TOP_K) * t + tok

    def w_in_map(i, f, ie, ir, inn, tk, ds_):
        return (layer, ie[i], 0, jnp.where(inn[i] > 0, f, nf - 1))

    def w_out_map(i, f, ie, ir, inn, tk, ds_):
        return (layer, ie[i], jnp.where(inn[i] > 0, f, nf - 1), 0)

    grid_spec = pltpu.PrefetchScalarGridSpec(
        num_scalar_prefetch=5,
        grid=(n_items, nf),
        in_specs=[pl.BlockSpec(memory_space=pl.ANY),
                  pl.BlockSpec((None, None, d, fc), w_in_map),
                  pl.BlockSpec((None, None, d, fc), w_in_map),
                  pl.BlockSpec((None, None, fc, d), w_out_map)],
        out_specs=pl.BlockSpec(memory_space=pl.ANY),
        scratch_shapes=[pltpu.VMEM((2, rb, d // 2), jnp.uint32), pltpu.VMEM((rb, d), F32),
                        pltpu.SemaphoreType.DMA((2,)), pltpu.SemaphoreType.DMA(())],
    )
    y = pl.pallas_call(
        _moe_kernel,
        grid_spec=grid_spec,
        out_shape=jax.ShapeDtypeStruct((a, d), F32),
        compiler_params=_params("arbitrary", "arbitrary"),
    )(item_e, item_row0, item_rows, tok, dst, xn, w_gate, w_up, w_down)
    return y.reshape(MOE_TOP_K, t, d)


def _combine_norm_kernel(h_ref, y0_ref, y1_ref, gate_ref, g_ref, ho_ref, a_ref):
    h = h_ref[...] + (y0_ref[...] * gate_ref[:, 0:1] + y1_ref[...] * gate_ref[:, 1:2])
    ho_ref[...] = h
    a_ref[...] = _rms(h, g_ref[...]).astype(a_ref.dtype)


def _combine_norm(h, y2, gates, g, layer, tm=ROW_TILE // 2):
    t, d = h.shape
    return pl.pallas_call(
        _combine_norm_kernel,
        grid=_row_grid(t, tm),
        in_specs=[pl.BlockSpec((tm, d), lambda i: (i, 0)),
                  pl.BlockSpec((None, tm, d), lambda i: (0, i, 0)),
                  pl.BlockSpec((None, tm, d), lambda i: (1, i, 0)),
                  pl.BlockSpec((tm, MOE_TOP_K), lambda i: (i, 0)),
                  pl.BlockSpec((None, 1, d), lambda i: (layer, 0, 0))],
        out_specs=[pl.BlockSpec((tm, d), lambda i: (i, 0)), pl.BlockSpec((tm, d), lambda i: (i, 0))],
        out_shape=[jax.ShapeDtypeStruct((t, d), F32), jax.ShapeDtypeStruct((t, d), BF16)],
        compiler_params=_params("parallel"),
    )(h, y2, y2, gates, g)


def kernel(x_prompt, x_sample, cache_swa_k, cache_swa_v, state_gla, p_prompt, p_sample, ln_mix, ln_ffn, ln_ple, ln_final, swa_w_in, swa_sinks, swa_w_out, gla_w_in, gla_w_gate_up, gla_b_gate, gla_norm, gla_w_out, moe_w_group, moe_b_group, moe_w_expert, moe_b_expert, moe_w_gate, moe_w_up, moe_w_down, ple_w_proj, ple_w_gate):
    batch, seq, d = x_prompt.shape
    dec = x_sample.shape[0]
    assert x_sample.shape[1] == 1
    tp = batch * seq
    t = tp + dec
    depth = ln_mix.shape[0]
    kvh = cache_swa_k.shape[3]
    kvd = kvh * SWA_HEAD_DIM
    dk = gla_w_gate_up.shape[-1]
    dv = (gla_w_in.shape[-1] - GLA_GATE_RANK - 2 * dk) // 2

    x_p, x_s = x_prompt.reshape(tp, d), x_sample.reshape(dec, d)
    h = None
    p_all = jnp.concatenate([p_prompt.reshape(depth, tp, -1), p_sample.reshape(depth, dec, -1)], axis=1)
    pad = ROUTER_LANES - MOE_GROUPS - MOE_EXPERTS
    w_router = jnp.concatenate([moe_w_group, moe_w_expert, jnp.zeros((depth, d, pad), F32)], axis=-1)
    b_router = jnp.concatenate([moe_b_group, moe_b_expert, jnp.zeros((depth, pad), F32)], axis=-1)
    ln_mix, ln_ffn, ln_ple = (g.reshape(depth, 1, d) for g in (ln_mix, ln_ffn, ln_ple))

    new_k_p, new_v_p, new_s_p, new_k_s, new_v_s, new_s_s = [], [], [], [], [], []
    for i in range(depth):
        j = i // 2
        if i % 2 == 0:
            if h is None:
                qkv = _matmul(_rmsnorm(x_p, ln_mix, i), swa_w_in, j, swa_w_in.shape[-1],
                              a_tail=_rmsnorm(x_s, ln_mix, i))
            else:
                qkv = _matmul(_rmsnorm(h, ln_mix, i), swa_w_in, j, swa_w_in.shape[-1])
            o, nk_p, nv_p = _swa_prompt(qkv, swa_sinks[j], batch, seq, kvh)
            o_s, nk, nv = _swa_decode(qkv, swa_sinks[j], cache_swa_k, cache_swa_v, j, tp)
            new_k_p.append(nk_p.reshape(1, batch, WINDOW, kvh, SWA_HEAD_DIM))
            new_v_p.append(nv_p.reshape(1, batch, WINDOW, kvh, SWA_HEAD_DIM))
            new_k_s.append(nk)
            new_v_s.append(nv)
            if h is None:
                h = _matmul(o, swa_w_out, j, d, mode="res", res=x_p, res_tail=x_s, a_tail=o_s.astype(BF16))
            else:
                h = _matmul(o, swa_w_out, j, d, mode="res", res=h, a_tail=o_s.astype(BF16))
        else:
            w_gl = lax.slice(gla_w_in, (j, 0, 2 * dk + 2 * dv), (j + 1, d, gla_w_in.shape[-1]))[0]
            a, gl = _rmsnorm_proj(h, ln_mix, i, w_gl)
            proj = _matmul(a, gla_w_in, j, 2 * dk + 2 * dv)
            o, s_p = _gla_prompt(proj, gl, gla_w_gate_up[j:j + 1], gla_b_gate[j:j + 1], gla_norm[j:j + 1], batch, seq, tp)
            log_a = _gla_gate(gl, gla_w_gate_up[j:j + 1], gla_b_gate[j:j + 1], tp, dec)
            o_s, s_s = _gla_decode(proj[tp:].reshape(dec, 1, -1), log_a, gla_norm[j:j + 1], state_gla, j)
            new_s_p.append(s_p)
            new_s_s.append(s_s)
            h = _matmul(o, gla_w_out, j, d, mode="res", res=h, a_tail=o_s.reshape(dec, dv).astype(BF16))
        xn, eid, gates = _rmsnorm_router(h, ln_ffn, i, w_router[i], b_router[i:i + 1], tp)
        y2 = _moe_experts(xn, eid, moe_w_gate, moe_w_up, moe_w_down, i)
        h, a3 = _combine_norm(h, y2, gates, ln_ple, i)
        h = _matmul(a3, ple_w_gate, i, d, mode="ple", res=h, p=p_all, wp=ple_w_proj)

    g_fin = ln_final.reshape(1, 1, d)
    y_p = _rmsnorm(h, g_fin, 0, out_dtype=F32, rows=tp)
    y_s = _rmsnorm(h, g_fin, 0, out_dtype=F32, rows=dec, row_block0=tp // dec, tm=dec)
    return (y_p.reshape(batch, seq, d), y_s.reshape(dec, 1, d),
            *(per_layer[0] if len(per_layer) == 1 else jnp.concatenate(per_layer)
              for per_layer in (new_k_p, new_v_p, new_s_p, new_k_s, new_v_s, new_s_s)))
```

```python
import functools

import jax
import jax.numpy as jnp
from jax import lax
from jax.experimental import pallas as pl
from jax.experimental.pallas import tpu as pltpu

F32 = jnp.float32
BF16 = jnp.bfloat16

EPS = 1e-6
WINDOW = 128
SWA_HEAD_DIM = 128
GLA_HEADS = 4
GLA_GATE_RANK = 16
GLA_TAU = 16.0
GLA_CHUNK = 256
MOE_GROUPS = 8
MOE_EXPERTS_PER_GROUP = 8
MOE_EXPERTS = MOE_GROUPS * MOE_EXPERTS_PER_GROUP
MOE_TOP_K = 2
ROUTER_LANES = 128

V7X_VMEM_LIMIT_BYTES = 60 * 1024 * 1024
ROW_TILE = 512
MM_ROW_TILE = 1024
COL_TILE = 512
MOE_ROW_BLOCK = 512
MOE_ROW_SIZES = (256, 384, 512)
MOE_FF_TILE = 256
CAST_ROWS = 256


def _params(*sem):
    return pltpu.CompilerParams(dimension_semantics=sem, vmem_limit_bytes=V7X_VMEM_LIMIT_BYTES)


def _sigmoid(x):
    return 1.0 / (1.0 + jnp.exp(-x))


def _cast_rows(src_ref, dst_ref):
    rows = src_ref.shape[0]
    step = min(CAST_ROWS, rows)

    def body(i, c):
        r = pl.multiple_of(i * step, step)
        dst_ref[pl.ds(r, step), :] = src_ref[pl.ds(r, step), :].astype(BF16)
        return c

    lax.fori_loop(0, rows // step, body, 0)


def _cast_cols_transposed(src_ref, dst_ref):
    k = src_ref.shape[1]
    step = min(CAST_ROWS, k)

    def body(i, c):
        r = pl.multiple_of(i * step, step)
        dst_ref[pl.ds(r, step), :] = src_ref[:, pl.ds(r, step)].T.astype(BF16)
        return c

    lax.fori_loop(0, k // step, body, 0)


def _rms(x, g):
    ms = jnp.mean(x * x, axis=-1, keepdims=True)
    return x * lax.rsqrt(ms + EPS) * g


def _norm_kernel(h_ref, g_ref, o_ref):
    o_ref[...] = _rms(h_ref[...], g_ref[...]).astype(o_ref.dtype)


def _norm_proj_kernel(h_ref, g_ref, w_ref, o_ref, s_ref):
    y = _rms(h_ref[...], g_ref[...]).astype(BF16)
    o_ref[...] = y
    s_ref[...] = lax.dot_general(y, w_ref[...].astype(BF16), _NT, preferred_element_type=F32)


def _route(logits):
    lane = lax.broadcasted_iota(jnp.int32, logits.shape, 1)
    neg = jnp.float32(-jnp.inf)
    is_g = lane < MOE_GROUPS
    lg = jnp.where(is_g, logits, neg)
    mg = jnp.max(lg, axis=-1, keepdims=True)
    g_sel = jnp.min(jnp.where(lg == mg, lane, ROUTER_LANES), axis=-1, keepdims=True)
    p_g = 1.0 / jnp.sum(jnp.where(is_g, jnp.exp(lg - mg), 0.0), axis=-1, keepdims=True)
    lo = MOE_GROUPS + g_sel * MOE_EXPERTS_PER_GROUP
    in_grp = (lane >= lo) & (lane < lo + MOE_EXPERTS_PER_GROUP)
    le = jnp.where(in_grp, logits, neg)
    v0 = jnp.max(le, axis=-1, keepdims=True)
    i0 = jnp.min(jnp.where(le == v0, lane, ROUTER_LANES), axis=-1, keepdims=True)
    le1 = jnp.where(lane == i0, neg, le)
    v1 = jnp.max(le1, axis=-1, keepdims=True)
    i1 = jnp.min(jnp.where(le1 == v1, lane, ROUTER_LANES), axis=-1, keepdims=True)
    e1 = jnp.exp(v1 - v0)
    g0 = p_g / (1.0 + e1)
    g1 = p_g * e1 / (1.0 + e1)
    return i0 - MOE_GROUPS, i1 - MOE_GROUPS, g0, g1


def _norm_router_kernel(h_ref, g_ref, w_ref, b_ref, o_ref, eid_ref, gate_ref):
    y = _rms(h_ref[...], g_ref[...])
    o_ref[...] = _pack_bf16_halves(y)
    logits = jnp.dot(y.astype(BF16), w_ref[...].astype(BF16), preferred_element_type=F32)
    e0, e1, g0, g1 = _route(logits + b_ref[...])
    eid_ref[:, 0:1] = e0
    eid_ref[:, 1:2] = e1
    gate_ref[:, 0:1] = g0
    gate_ref[:, 1:2] = g1


def _row_grid(t, tm):
    return (pl.cdiv(t, tm),)


def _rmsnorm(h, g, layer, out_dtype=BF16, rows=None, row_block0=0, tm=ROW_TILE):
    t, d = h.shape
    rows = t if rows is None else rows
    tm = min(tm, rows)
    return pl.pallas_call(
        _norm_kernel,
        grid=_row_grid(rows, tm),
        in_specs=[pl.BlockSpec((tm, d), lambda i: (row_block0 + i, 0)),
                  pl.BlockSpec((None, 1, d), lambda i: (layer, 0, 0))],
        out_specs=pl.BlockSpec((tm, d), lambda i: (i, 0)),
        out_shape=jax.ShapeDtypeStruct((rows, d), out_dtype),
        compiler_params=_params("parallel"),
    )(h, g)


def _rmsnorm_proj(h, g, layer, w_t, w_layer, col0, ns, tm=ROW_TILE):
    t, d = h.shape
    assert col0 % ns == 0 and ns % 8 == 0
    return pl.pallas_call(
        _norm_proj_kernel,
        grid=_row_grid(t, tm),
        in_specs=[pl.BlockSpec((tm, d), lambda i: (i, 0)),
                  pl.BlockSpec((None, 1, d), lambda i: (layer, 0, 0)),
                  pl.BlockSpec((None, ns, d), lambda i: (w_layer, col0 // ns, 0))],
        out_specs=[pl.BlockSpec((tm, d), lambda i: (i, 0)),
                   pl.BlockSpec((tm, ns), lambda i: (i, 0))],
        out_shape=[jax.ShapeDtypeStruct((t, d), BF16), jax.ShapeDtypeStruct((t, ns), F32)],
        compiler_params=_params("parallel"),
    )(h, g, w_t)


def _rmsnorm_router(h, g, layer, w_router, b_router, tm=ROW_TILE // 2):
    t, d = h.shape
    return pl.pallas_call(
        _norm_router_kernel,
        grid=_row_grid(t, tm),
        in_specs=[pl.BlockSpec((tm, d), lambda i: (i, 0)),
                  pl.BlockSpec((None, 1, d), lambda i: (layer, 0, 0)),
                  pl.BlockSpec((d, ROUTER_LANES), lambda i: (0, 0)),
                  pl.BlockSpec((1, ROUTER_LANES), lambda i: (0, 0))],
        out_specs=[pl.BlockSpec((tm, d // 2), lambda i: (i, 0)),
                   pl.BlockSpec((tm, MOE_TOP_K), lambda i: (i, 0)),
                   pl.BlockSpec((tm, MOE_TOP_K), lambda i: (i, 0))],
        out_shape=[jax.ShapeDtypeStruct((t, d // 2), jnp.uint32),
                   jax.ShapeDtypeStruct((t, MOE_TOP_K), jnp.int32),
                   jax.ShapeDtypeStruct((t, MOE_TOP_K), F32)],
        compiler_params=_params("parallel"),
    )(h, g, w_router, b_router)


def _mm_kernel(*refs, mode, tail, split_a, split_r, w_cols_major):
    a_ref, refs = refs[0], refs[1:]
    at_ref = a_ref
    if split_a:
        at_ref, refs = refs[0], refs[1:]
    w_ref, refs = refs[0], refs[1:]
    r_ref = rt_ref = None
    if mode != "plain":
        r_ref, refs = refs[0], refs[1:]
        rt_ref = r_ref
        if split_r:
            rt_ref, refs = refs[0], refs[1:]
    if mode == "ple":
        p_ref, wp_ref, o_ref, wb_ref, wpb_ref = refs
    else:
        o_ref, wb_ref = refs
    m = pl.program_id(1)

    @pl.when(m == 0)
    def _():
        if w_cols_major:
            _cast_cols_transposed(w_ref, wb_ref)
        else:
            _cast_rows(w_ref, wb_ref)
        if mode == "ple":
            _cast_rows(wp_ref, wpb_ref)

    def tile(src_ref, res_ref, rows):
        acc = jnp.dot(src_ref[rows, :], wb_ref[...], preferred_element_type=F32)
        if mode == "plain":
            o_ref[rows, :] = acc.astype(o_ref.dtype)
        elif mode == "res":
            o_ref[rows, :] = res_ref[rows, :] + acc
        else:
            pp = jnp.dot(p_ref[rows, :].astype(BF16), wpb_ref[...], preferred_element_type=F32)
            o_ref[rows, :] = res_ref[rows, :] + pp * _sigmoid(acc)

    if tail == 0:
        tile(a_ref, r_ref, slice(None))
    else:
        last = pl.num_programs(1) - 1
        pl.when(m < last)(lambda: tile(a_ref, r_ref, slice(None)))
        pl.when(m == last)(lambda: tile(at_ref, rt_ref, slice(0, tail)))


def _matmul(a, w, layer, n_cols, mode="plain", res=None, p=None, wp=None, a_tail=None, res_tail=None,
            w_cols_major=False, out_dtype=F32, tm=MM_ROW_TILE, tn=COL_TILE):
    k = a.shape[1]
    tm = min(tm, a.shape[0] // 256 * 256)
    split_a = a_tail is not None
    split_r = res_tail is not None
    if split_a:
        assert a.shape[0] % tm == 0 and a_tail.shape[0] < tm
        t, tail = a.shape[0] + a_tail.shape[0], a_tail.shape[0]
    else:
        t, tail = a.shape[0], a.shape[0] % tm
    if split_r:
        assert res.shape[0] == t - tail and res_tail.shape[0] == tail
    n_full = t // tm

    def full_rows(n, m):
        return jnp.minimum(m, n_full - 1)

    tn = min(tn, n_cols)
    assert n_cols % tn == 0
    grid = (n_cols // tn, pl.cdiv(t, tm))
    if split_a:
        in_specs = [pl.BlockSpec((tm, k), lambda n, m: (full_rows(n, m), 0)),
                    pl.BlockSpec((tail, k), lambda n, m: (0, 0))]
        args = [a, a_tail]
    else:
        in_specs = [pl.BlockSpec((tm, k), lambda n, m: (m, 0))]
        args = [a]
    if w_cols_major:
        in_specs.append(pl.BlockSpec((None, tn, k), lambda n, m: (layer, n, 0)))
    else:
        in_specs.append(pl.BlockSpec((None, k, tn), lambda n, m: (layer, 0, n)))
    scratch = [pltpu.VMEM((k, tn), BF16)]
    args.append(w)
    if mode in ("res", "ple"):
        if split_r:
            in_specs += [pl.BlockSpec((tm, tn), lambda n, m: (full_rows(n, m), n)),
                         pl.BlockSpec((tail, tn), lambda n, m: (0, n))]
            args += [res, res_tail]
        else:
            in_specs.append(pl.BlockSpec((tm, tn), lambda n, m: (m, n)))
            args.append(res)
    if mode == "ple":
        kp = p.shape[-1]
        in_specs += [pl.BlockSpec((None, tm, kp), lambda n, m: (layer, m, 0)),
                     pl.BlockSpec((None, kp, tn), lambda n, m: (layer, 0, n))]
        args += [p, wp]
        scratch.append(pltpu.VMEM((kp, tn), BF16))
    return pl.pallas_call(
        functools.partial(_mm_kernel, mode=mode, tail=tail, split_a=split_a, split_r=split_r,
                          w_cols_major=w_cols_major),
        grid=grid,
        in_specs=in_specs,
        out_specs=pl.BlockSpec((tm, tn), lambda n, m: (m, n)),
        out_shape=jax.ShapeDtypeStruct((t, n_cols), out_dtype),
        scratch_shapes=scratch,
        compiler_params=_params("arbitrary", "arbitrary"),
    )(*args)


def _softmax_sink_pv(s, sk, vb):
    m = jnp.maximum(jnp.max(s, axis=-1, keepdims=True), sk)
    p = jnp.exp(s - m)
    den = jnp.sum(p, axis=-1, keepdims=True) + jnp.exp(sk - m)
    return jnp.dot((p / den).astype(BF16), vb, preferred_element_type=F32)


_NT = (((1,), (1,)), ((), ()))


def _swa_prompt_kernel(sinks_ref, q_ref, kp_ref, kc_ref, vp_ref, vc_ref, o_ref, nk_ref, nv_ref, *, kvh, group):
    n = pl.program_id(1)
    w, kvd = kc_ref.shape
    hd = kvd // kvh
    kk = jnp.concatenate([kp_ref[...], kc_ref[...]], axis=0).astype(BF16)
    vv = jnp.concatenate([vp_ref[...], vc_ref[...]], axis=0).astype(BF16)
    qi = lax.broadcasted_iota(jnp.int32, (w, 2 * w), 0)
    kj = lax.broadcasted_iota(jnp.int32, (w, 2 * w), 1)
    mask = (kj > qi) & (kj <= qi + w) & (kj >= jnp.where(n > 0, 0, w))
    scale = hd ** -0.5
    for kv in range(kvh):
        kh = kk[:, kv * hd:(kv + 1) * hd]
        vh = vv[:, kv * hd:(kv + 1) * hd]
        for g in range(group):
            c0 = (kv * group + g) * hd
            qg = (q_ref[:, c0:c0 + hd] * scale).astype(BF16)
            s = lax.dot_general(qg, kh, _NT, preferred_element_type=F32)
            s = jnp.where(mask, s, -jnp.inf)
            o = _softmax_sink_pv(s, sinks_ref[kv * group + g], vh)
            o_ref[:, c0:c0 + hd] = o.astype(o_ref.dtype)

    @pl.when(n == pl.num_programs(1) - 1)
    def _():
        nk_ref[...] = kc_ref[...]
        nv_ref[...] = vc_ref[...]


def _swa_prompt(qkv, sinks, batch, seq, kvh):
    hd, w = SWA_HEAD_DIM, WINDOW
    kvd = kvh * hd
    qd = qkv.shape[1] - 2 * kvd
    group = qd // kvd
    nb = seq // w
    kcol = qd // kvd

    def cur(b, n):
        return b * nb + n

    def prev(b, n):
        return b * nb + jnp.maximum(n - 1, 0)

    last_spec = pl.BlockSpec((None, None, w, kvd), lambda b, n: (0, b, 0, 0))
    return pl.pallas_call(
        functools.partial(_swa_prompt_kernel, kvh=kvh, group=group),
        grid=(batch, nb),
        in_specs=[pl.BlockSpec(memory_space=pltpu.SMEM),
                  pl.BlockSpec((w, qd), lambda b, n: (cur(b, n), 0)),
                  pl.BlockSpec((w, kvd), lambda b, n: (prev(b, n), kcol)),
                  pl.BlockSpec((w, kvd), lambda b, n: (cur(b, n), kcol)),
                  pl.BlockSpec((w, kvd), lambda b, n: (prev(b, n), kcol + 1)),
                  pl.BlockSpec((w, kvd), lambda b, n: (cur(b, n), kcol + 1))],
        out_specs=[pl.BlockSpec((w, qd), lambda b, n: (cur(b, n), 0)), last_spec, last_spec],
        out_shape=[jax.ShapeDtypeStruct((batch * seq, qd), BF16),
                   jax.ShapeDtypeStruct((1, batch, w, kvd), F32),
                   jax.ShapeDtypeStruct((1, batch, w, kvd), F32)],
        compiler_params=_params("parallel", "arbitrary"),
    )(sinks, qkv, qkv, qkv, qkv, qkv)


def _swa_decode_kernel(sk_ref, q_ref, x_ref, kp_ref, vp_ref, nk_ref, nv_ref, o_ref, *, kvh, group):
    bb, w, kvd = kp_ref.shape
    hd = kvd // kvh
    nh = kvh * group
    qd = nh * hd
    scale = hd ** -0.5
    row = lax.broadcasted_iota(jnp.int32, (w, kvd), 0)
    head_kv = lax.broadcasted_iota(jnp.int32, (nh, hd), 0) // group
    sk = sk_ref[...]
    for bi in range(bb):
        x = x_ref[bi:bi + 1, :]
        kc = jnp.where(row == w - 1, x[:, qd:qd + kvd], pltpu.roll(kp_ref[bi], w - 1, axis=0))
        vc = jnp.where(row == w - 1, x[:, qd + kvd:], pltpu.roll(vp_ref[bi], w - 1, axis=0))
        nk_ref[bi] = kc
        nv_ref[bi] = vc
        s = lax.dot_general((q_ref[bi] * scale).astype(BF16), kc.astype(BF16), _NT, preferred_element_type=F32)
        ov = _softmax_sink_pv(s, sk, vc.astype(BF16))
        o = jnp.zeros((nh, hd), F32)
        for kv in range(kvh):
            o = jnp.where(head_kv == kv, ov[:, kv * hd:(kv + 1) * hd], o)
        o_ref[bi] = o


def _swa_decode(qkv, sinks, k_past, v_past, layer, row0, bb=8):
    nl, b, w, kvh, hd = k_past.shape
    kvd = kvh * hd
    qd = qkv.shape[1] - 2 * kvd
    group = qd // kvd
    nh = kvh * group
    bb = min(bb, b)
    blk0 = row0 // bb
    q = qkv[row0:row0 + b, :qd].reshape(b, nh, 1, hd)
    own = (jnp.arange(nh)[:, None] // group == jnp.arange(kvh)[None, :])[None, :, :, None]
    q_blocks = jnp.where(own, q, 0.0).reshape(b, nh, kvd)
    in_spec = pl.BlockSpec((None, bb, w, kvd), lambda i: (layer, i, 0, 0))
    out_spec = pl.BlockSpec((None, bb, w, kvd), lambda i: (0, i, 0, 0))
    nk, nv, o = pl.pallas_call(
        functools.partial(_swa_decode_kernel, kvh=kvh, group=group),
        grid=(b // bb,),
        in_specs=[pl.BlockSpec((nh, 1), lambda i: (0, 0)),
                  pl.BlockSpec((bb, nh, kvd), lambda i: (i, 0, 0)),
                  pl.BlockSpec((bb, qkv.shape[1]), lambda i: (blk0 + i, 0)),
                  in_spec, in_spec],
        out_specs=[out_spec, out_spec, pl.BlockSpec((bb, nh, hd), lambda i: (i, 0, 0))],
        out_shape=[jax.ShapeDtypeStruct((1, b, w, kvd), F32), jax.ShapeDtypeStruct((1, b, w, kvd), F32),
                   jax.ShapeDtypeStruct((b, nh, hd), F32)],
        compiler_params=_params("parallel"),
    )(sinks.reshape(nh, 1), q_blocks, qkv, k_past.reshape(nl, b, w, kvd), v_past.reshape(nl, b, w, kvd))
    return o.reshape(b, qd), nk.reshape(1, b, w, kvh, hd), nv.reshape(1, b, w, kvh, hd)


def _log_decay(gl, wgu, bg):
    x = jnp.dot(gl.astype(BF16), wgu.astype(BF16), preferred_element_type=F32) + bg
    return (jnp.minimum(x, 0.0) - jnp.log(1.0 + jnp.exp(-jnp.abs(x)))) * (1.0 / GLA_TAU)


def _gla_out(o, ng, r):
    return _rms(o, ng) * (r * _sigmoid(r))


def _block_ref_row(x, row, s):
    c, d = x.shape
    if 2 * s >= 8:
        nblk = c // (2 * s)
        ref = x.reshape(nblk, 2 * s, d)[:, s - 1:s, :]
        return jnp.broadcast_to(ref, (nblk, 2 * s, d)).reshape(c, d)
    pos = row & (2 * s - 1)
    out = x
    for delta in range(s - 1, -s - 1, -1):
        if delta != 0:
            out = jnp.where(pos == s - 1 - delta, pltpu.roll(x, (c - delta) % c, axis=0), out)
    return out


def _gla_prompt_kernel(q_ref, k_ref, v_ref, r_ref, gl_ref, wgu_ref, bg_ref, ng_ref, o_ref, s_ref):
    c, dkh = q_ref.shape

    @pl.when(pl.program_id(2) == 0)
    def _():
        s_ref[...] = jnp.zeros_like(s_ref)

    g = _log_decay(gl_ref[...], wgu_ref[...], bg_ref[...])
    row = lax.broadcasted_iota(jnp.int32, (c, dkh), 0)
    cum = g
    sh = 1
    while sh < c:
        cum = cum + jnp.where(row >= sh, pltpu.roll(cum, sh, axis=0), 0.0)
        sh *= 2
    q = q_ref[...] * (dkh ** -0.5)
    k = k_ref[...]
    vb = v_ref[...].astype(BF16)

    ri = lax.broadcasted_iota(jnp.int32, (c, c), 0)
    ci = lax.broadcasted_iota(jnp.int32, (c, c), 1)
    att = jnp.where(ri == ci, lax.dot_general(q.astype(BF16), k.astype(BF16), _NT, preferred_element_type=F32), 0.0)
    s = c // 2
    while s >= 1:
        cref = _block_ref_row(cum, row, s)
        e = jnp.exp(jnp.where((row & s) != 0, cum - cref, cref - cum))
        a_s = lax.dot_general((q * e).astype(BF16), (k * e).astype(BF16), _NT, preferred_element_type=F32)
        pair = (((ri ^ ci) >> (s.bit_length() - 1)) == 1) & ((ri & s) != 0)
        att = att + jnp.where(pair, a_s, 0.0)
        s //= 2

    st = s_ref[...]
    o = jnp.dot((q * jnp.exp(cum)).astype(BF16), st.astype(BF16), preferred_element_type=F32)
    o = o + jnp.dot(att.astype(BF16), vb, preferred_element_type=F32)
    o_ref[...] = _gla_out(o, ng_ref[...], r_ref[...]).astype(o_ref.dtype)

    last = cum[c - 1:c, :]
    kd_t = (k * jnp.exp(last - cum)).T.astype(BF16)
    e_col = jnp.exp(cum.T[:, c - 1:c])
    s_ref[...] = e_col * st + jnp.dot(kd_t, vb, preferred_element_type=F32)


def _gla_prompt(proj, gl, w_gate_up, b_gate, norm_g, batch, seq, t_out):
    h = GLA_HEADS
    dk = w_gate_up.shape[-1]
    dkh = dk // h
    dv = (proj.shape[1] - 2 * dk) // 2
    dvh = dv // h
    c = GLA_CHUNK
    nc = seq // c

    def rows(b, hh, i):
        return b * nc + i

    return pl.pallas_call(
        _gla_prompt_kernel,
        grid=(batch, h, nc),
        in_specs=[pl.BlockSpec((c, dkh), lambda b, hh, i: (rows(b, hh, i), hh)),
                  pl.BlockSpec((c, dkh), lambda b, hh, i: (rows(b, hh, i), h + hh)),
                  pl.BlockSpec((c, dvh), lambda b, hh, i: (rows(b, hh, i), 2 * dk // dvh + hh)),
                  pl.BlockSpec((c, dvh), lambda b, hh, i: (rows(b, hh, i), (2 * dk + dv) // dvh + hh)),
                  pl.BlockSpec((c, GLA_GATE_RANK), lambda b, hh, i: (rows(b, hh, i), 0)),
                  pl.BlockSpec((None, GLA_GATE_RANK, dkh), lambda b, hh, i: (0, 0, hh)),
                  pl.BlockSpec((1, dkh), lambda b, hh, i: (0, hh)),
                  pl.BlockSpec((1, dvh), lambda b, hh, i: (0, 0))],
        out_specs=[pl.BlockSpec((c, dvh), lambda b, hh, i: (rows(b, hh, i), hh)),
                   pl.BlockSpec((None, None, None, dkh, dvh), lambda b, hh, i: (0, b, hh, 0, 0))],
        out_shape=[jax.ShapeDtypeStruct((t_out, dv), BF16),
                   jax.ShapeDtypeStruct((1, batch, h, dkh, dvh), F32)],
        compiler_params=_params("parallel", "parallel", "arbitrary"),
    )(proj, proj, proj, proj, gl, w_gate_up, b_gate, norm_g)


def _gla_gate_kernel(gl_ref, wgu_ref, bg_ref, o_ref):
    o_ref[...] = _log_decay(gl_ref[...], wgu_ref[...], bg_ref[...])


def _gla_gate(gl, w_gate_up, b_gate, row0, rows):
    dk = w_gate_up.shape[-1]
    return pl.pallas_call(
        _gla_gate_kernel,
        grid=(1,),
        in_specs=[pl.BlockSpec((rows, GLA_GATE_RANK), lambda i: (row0 // rows, 0)),
                  pl.BlockSpec((None, GLA_GATE_RANK, dk), lambda i: (0, 0, 0)),
                  pl.BlockSpec((1, dk), lambda i: (0, 0))],
        out_specs=pl.BlockSpec((rows, dk), lambda i: (0, 0)),
        out_shape=jax.ShapeDtypeStruct((rows, dk), F32),
        compiler_params=_params("arbitrary"),
    )(gl, w_gate_up, b_gate)


def _gla_decode_kernel(q_ref, k_ref, g_ref, v_ref, r_ref, ng_ref, s_ref, o_ref, ns_ref):
    dkh = q_ref.shape[-1]
    rows = jnp.concatenate([q_ref[...], k_ref[...], g_ref[...], jnp.zeros((128 - 3, dkh), F32)], axis=0)
    cols = rows.T
    q_col, k_col, g_col = cols[:, 0:1], cols[:, 1:2], cols[:, 2:3]
    sn = jnp.exp(g_col) * s_ref[...] + k_col * v_ref[...]
    ns_ref[...] = sn
    o = jnp.sum((q_col * (dkh ** -0.5)) * sn, axis=0, keepdims=True)
    o_ref[...] = _gla_out(o, ng_ref[...], r_ref[...])


def _gla_decode(proj_s, log_a, norm_g, state, layer):
    b, h, dkh, dvh = state.shape[1:]
    dk, dv = h * dkh, h * dvh
    st_spec = pl.BlockSpec((None, None, None, dkh, dvh), lambda i, hh: (layer, i, hh, 0, 0))
    st_out_spec = pl.BlockSpec((None, None, None, dkh, dvh), lambda i, hh: (0, i, hh, 0, 0))
    return pl.pallas_call(
        _gla_decode_kernel,
        grid=(b, h),
        in_specs=[pl.BlockSpec((None, 1, dkh), lambda i, hh: (i, 0, hh)),
                  pl.BlockSpec((None, 1, dkh), lambda i, hh: (i, 0, h + hh)),
                  pl.BlockSpec((None, 1, dkh), lambda i, hh: (i, 0, hh)),
                  pl.BlockSpec((None, 1, dvh), lambda i, hh: (i, 0, 2 * dk // dvh + hh)),
                  pl.BlockSpec((None, 1, dvh), lambda i, hh: (i, 0, (2 * dk + dv) // dvh + hh)),
                  pl.BlockSpec((1, dvh), lambda i, hh: (0, 0)),
                  st_spec],
        out_specs=[pl.BlockSpec((None, 1, dvh), lambda i, hh: (i, 0, hh)), st_out_spec],
        out_shape=[jax.ShapeDtypeStruct((b, 1, dv), F32), jax.ShapeDtypeStruct((1, b, h, dkh, dvh), F32)],
        compiler_params=_params("parallel", "parallel"),
    )(proj_s, proj_s, log_a.reshape(b, 1, dk), proj_s, proj_s, norm_g, state)


def _pack_bf16_halves(y):
    half = y.shape[1] // 2
    bits = lax.bitcast_convert_type(y.astype(BF16).astype(F32), jnp.uint32)
    return (bits[:, :half] >> 16) | (bits[:, half:] & jnp.uint32(0xFFFF0000))


def _unpack_bf16_halves(u):
    lo = lax.bitcast_convert_type(u << 16, F32)
    hi = lax.bitcast_convert_type(u & jnp.uint32(0xFFFF0000), F32)
    return jnp.concatenate([lo, hi], axis=1).astype(BF16)


def _for_rows(n, fn, unroll=8):
    def group(j, c):
        for u in range(unroll):
            fn(j * unroll + u)
        return c

    def single(r, c):
        fn(r)
        return c

    full = n // unroll
    lax.fori_loop(0, full, group, 0)
    lax.fori_loop(full * unroll, n, single, 0)


def _moe_kernel(ie_ref, ir_ref, in_ref, tok_ref, dst_ref, x_hbm, wg_ref, wu_ref, wd_ref, y_hbm,
                xbuf, ybuf, gsem, ssem):
    i = pl.program_id(0)
    f = pl.program_id(1)
    ni = pl.num_programs(0)
    nf = pl.num_programs(1)
    n = in_ref[i]
    slot = i % 2
    rb = ybuf.shape[0]
    sizes = sorted({min(m, rb) for m in MOE_ROW_SIZES})

    def gather(item, sl, r):
        src = x_hbm.at[pl.ds(tok_ref[ir_ref[item] + r], 1)]
        return pltpu.make_async_copy(src, xbuf.at[sl, pl.ds(r, 1)], gsem.at[sl])

    def scatter(item, r):
        dst = y_hbm.at[pl.ds(dst_ref[ir_ref[item] + r], 1)]
        return pltpu.make_async_copy(ybuf.at[pl.ds(r, 1)], dst, ssem)

    @pl.when(f == 0)
    def _():
        @pl.when(i == 0)
        def _():
            xbuf[...] = jnp.zeros_like(xbuf)
            ybuf[...] = jnp.zeros_like(ybuf)
            _for_rows(n, lambda r: gather(0, 0, r).start())

        nxt = jnp.minimum(i + 1, ni - 1)
        n_nxt = jnp.where(i + 1 < ni, in_ref[nxt], 0)
        _for_rows(n_nxt, lambda r: gather(nxt, 1 - slot, r).start())
        _for_rows(n, lambda r: gather(i, slot, r).wait())
        prv = jnp.maximum(i - 1, 0)
        n_prv = jnp.where(i > 0, in_ref[prv], 0)
        _for_rows(n_prv, lambda r: scatter(prv, r).wait())

    lo = 0
    for m in sizes:
        @pl.when((n > lo) & (n <= m))
        def _(m=m):
            xs = _unpack_bf16_halves(xbuf[slot, :m, :])
            hg = jnp.dot(xs, wg_ref[...].astype(BF16), preferred_element_type=F32)
            hu = jnp.dot(xs, wu_ref[...].astype(BF16), preferred_element_type=F32)
            act = (hg * _sigmoid(hg) * hu).astype(BF16)
            y = jnp.dot(act, wd_ref[...].astype(BF16), preferred_element_type=F32)
            ybuf[:m, :] = y + jnp.where(f > 0, ybuf[:m, :], 0.0)
        lo = m

    @pl.when(f == nf - 1)
    def _():
        _for_rows(n, lambda r: scatter(i, r).start())

        @pl.when(i == ni - 1)
        def _():
            _for_rows(n, lambda r: scatter(i, r).wait())


def _moe_experts(xn, eid, w_gate, w_up, w_down, layer):
    t, d = xn.shape[0], xn.shape[1] * 2
    n_exp, _, dff = w_gate.shape[1:]
    a = t * MOE_TOP_K
    rb = min(MOE_ROW_BLOCK, -(-a // 8) * 8)
    fc = min(MOE_FF_TILE, dff)
    nf = dff // fc
    n_items = n_exp + a // rb

    flat = eid.reshape(-1)
    order = jnp.argsort(flat, stable=True).astype(jnp.int32)
    counts = jnp.sum((flat[:, None] == jnp.arange(n_exp, dtype=jnp.int32)[None, :]).astype(jnp.int32), axis=0)
    starts = jnp.cumsum(counts) - counts
    per_e = (counts + rb - 1) // rb
    item_end = jnp.cumsum(per_e)
    item_start = item_end - per_e
    ii = jnp.arange(n_items, dtype=jnp.int32)
    valid = ii < item_end[-1]
    e_of = jnp.minimum(jnp.searchsorted(item_end, ii, side="right").astype(jnp.int32), n_exp - 1)
    local = ii - item_start[e_of]
    item_rows = jnp.where(valid, jnp.clip(counts[e_of] - local * rb, 0, rb), 0).astype(jnp.int32)
    item_row0 = jnp.where(valid, starts[e_of] + local * rb, 0).astype(jnp.int32)
    e_last = jnp.max(jnp.where(valid, e_of, 0))
    item_e = jnp.where(valid, e_of, e_last).astype(jnp.int32)
    tok = order // MOE_TOP_K
    dst = (order % MOE_TOP_K) * t + tok

    def w_in_map(i, f, ie, ir, inn, tk, ds_):
        return (layer, ie[i], 0, jnp.where(inn[i] > 0, f, nf - 1))

    def w_out_map(i, f, ie, ir, inn, tk, ds_):
        return (layer, ie[i], jnp.where(inn[i] > 0, f, nf - 1), 0)

    grid_spec = pltpu.PrefetchScalarGridSpec(
        num_scalar_prefetch=5,
        grid=(n_items, nf),
        in_specs=[pl.BlockSpec(memory_space=pl.ANY),
                  pl.BlockSpec((None, None, d, fc), w_in_map),
                  pl.BlockSpec((None, None, d, fc), w_in_map),
                  pl.BlockSpec((None, None, fc, d), w_out_map)],
        out_specs=pl.BlockSpec(memory_space=pl.ANY),
        scratch_shapes=[pltpu.VMEM((2, rb, d // 2), jnp.uint32), pltpu.VMEM((rb, d), F32),
                        pltpu.SemaphoreType.DMA((2,)), pltpu.SemaphoreType.DMA(())],
    )
    y = pl.pallas_call(
        _moe_kernel,
        grid_spec=grid_spec,
        out_shape=jax.ShapeDtypeStruct((a, d), F32),
        compiler_params=_params("arbitrary", "arbitrary"),
    )(item_e, item_row0, item_rows, tok, dst, xn, w_gate, w_up, w_down)
    return y.reshape(MOE_TOP_K, t, d)


def _combine_norm_kernel(h_ref, y0_ref, y1_ref, gate_ref, g_ref, ho_ref, a_ref):
    h = h_ref[...] + (y0_ref[...] * gate_ref[:, 0:1] + y1_ref[...] * gate_ref[:, 1:2])
    ho_ref[...] = h
    a_ref[...] = _rms(h, g_ref[...]).astype(a_ref.dtype)


def _combine_norm(h, y2, gates, g, layer, tm=ROW_TILE // 2):
    t, d = h.shape
    return pl.pallas_call(
        _combine_norm_kernel,
        grid=_row_grid(t, tm),
        in_specs=[pl.BlockSpec((tm, d), lambda i: (i, 0)),
                  pl.BlockSpec((None, tm, d), lambda i: (0, i, 0)),
                  pl.BlockSpec((None, tm, d), lambda i: (1, i, 0)),
                  pl.BlockSpec((tm, MOE_TOP_K), lambda i: (i, 0)),
                  pl.BlockSpec((None, 1, d), lambda i: (layer, 0, 0))],
        out_specs=[pl.BlockSpec((tm, d), lambda i: (i, 0)), pl.BlockSpec((tm, d), lambda i: (i, 0))],
        out_shape=[jax.ShapeDtypeStruct((t, d), F32), jax.ShapeDtypeStruct((t, d), BF16)],
        compiler_params=_params("parallel"),
    )(h, y2, y2, gates, g)


def kernel(x_prompt, x_sample, cache_swa_k, cache_swa_v, state_gla, p_prompt, p_sample, ln_mix, ln_ffn, ln_ple, ln_final, swa_w_in, swa_sinks, swa_w_out, gla_w_in, gla_w_gate_up, gla_b_gate, gla_norm, gla_w_out, moe_w_group, moe_b_group, moe_w_expert, moe_b_expert, moe_w_gate, moe_w_up, moe_w_down, ple_w_proj, ple_w_gate):
    batch, seq, d = x_prompt.shape
    dec = x_sample.shape[0]
    assert x_sample.shape[1] == 1
    tp = batch * seq
    t = tp + dec
    depth = ln_mix.shape[0]
    kvh = cache_swa_k.shape[3]
    kvd = kvh * SWA_HEAD_DIM
    dk = gla_w_gate_up.shape[-1]
    dv = (gla_w_in.shape[-1] - GLA_GATE_RANK - 2 * dk) // 2

    x_p, x_s = x_prompt.reshape(tp, d), x_sample.reshape(dec, d)
    h = None
    p_all = jnp.concatenate([p_prompt.reshape(depth, tp, -1), p_sample.reshape(depth, dec, -1)], axis=1)
    pad = ROUTER_LANES - MOE_GROUPS - MOE_EXPERTS
    w_router = jnp.concatenate([moe_w_group, moe_w_expert, jnp.zeros((depth, d, pad), F32)], axis=-1)
    b_router = jnp.concatenate([moe_b_group, moe_b_expert, jnp.zeros((depth, pad), F32)], axis=-1)
    ln_mix, ln_ffn, ln_ple = (g.reshape(depth, 1, d) for g in (ln_mix, ln_ffn, ln_ple))

    new_k_p, new_v_p, new_s_p, new_k_s, new_v_s, new_s_s = [], [], [], [], [], []
    for i in range(depth):
        j = i // 2
        if i % 2 == 0:
            if h is None:
                qkv = _matmul(_rmsnorm(x_p, ln_mix, i), swa_w_in, j, swa_w_in.shape[-1],
                              a_tail=_rmsnorm(x_s, ln_mix, i))
            else:
                qkv = _matmul(_rmsnorm(h, ln_mix, i), swa_w_in, j, swa_w_in.shape[-1])
            o, nk_p, nv_p = _swa_prompt(qkv, swa_sinks[j], batch, seq, kvh)
            o_s, nk, nv = _swa_decode(qkv, swa_sinks[j], cache_swa_k, cache_swa_v, j, tp)
            new_k_p.append(nk_p.reshape(1, batch, WINDOW, kvh, SWA_HEAD_DIM))
            new_v_p.append(nv_p.reshape(1, batch, WINDOW, kvh, SWA_HEAD_DIM))
            new_k_s.append(nk)
            new_v_s.append(nv)
            if h is None:
                h = _matmul(o, swa_w_out, j, d, mode="res", res=x_p, res_tail=x_s, a_tail=o_s.astype(BF16))
            else:
                h = _matmul(o, swa_w_out, j, d, mode="res", res=h, a_tail=o_s.astype(BF16))
        else:
            w_in_t = jnp.swapaxes(gla_w_in, 1, 2)
            a, gl = _rmsnorm_proj(h, ln_mix, i, w_in_t, j, 2 * dk + 2 * dv, GLA_GATE_RANK)
            proj = _matmul(a, w_in_t, j, 2 * dk + 2 * dv, w_cols_major=True)
            o, s_p = _gla_prompt(proj, gl, gla_w_gate_up[j:j + 1], gla_b_gate[j:j + 1], gla_norm[j:j + 1], batch, seq, tp)
            log_a = _gla_gate(gl, gla_w_gate_up[j:j + 1], gla_b_gate[j:j + 1], tp, dec)
            o_s, s_s = _gla_decode(proj[tp:].reshape(dec, 1, -1), log_a, gla_norm[j:j + 1], state_gla, j)
            new_s_p.append(s_p)
            new_s_s.append(s_s)
            h = _matmul(o, gla_w_out, j, d, mode="res", res=h, a_tail=o_s.reshape(dec, dv).astype(BF16))
        xn, eid, gates = _rmsnorm_router(h, ln_ffn, i, w_router[i], b_router[i:i + 1])
        y2 = _moe_experts(xn, eid, moe_w_gate, moe_w_up, moe_w_down, i)
        h, a3 = _combine_norm(h, y2, gates, ln_ple, i)
        h = _matmul(a3, ple_w_gate, i, d, mode="ple", res=h, p=p_all, wp=ple_w_proj)

    g_fin = ln_final.reshape(1, 1, d)
    y_p = _rmsnorm(h, g_fin, 0, out_dtype=F32, rows=tp)
    y_s = _rmsnorm(h, g_fin, 0, out_dtype=F32, rows=dec, row_block0=tp // dec, tm=dec)
    return (y_p.reshape(batch, seq, d), y_s.reshape(dec, 1, d),
            *(per_layer[0] if len(per_layer) == 1 else jnp.concatenate(per_layer)
              for per_layer in (new_k_p, new_v_p, new_s_p, new_k_s, new_v_s, new_s_s)))
```

```python
import functools

import jax
import jax.numpy as jnp
from jax import lax
from jax.experimental import pallas as pl
from jax.experimental.pallas import tpu as pltpu

F32 = jnp.float32
BF16 = jnp.bfloat16

EPS = 1e-6
WINDOW = 128
SWA_HEAD_DIM = 128
GLA_HEADS = 4
GLA_GATE_RANK = 16
GLA_TAU = 16.0
GLA_CHUNK = 256
MOE_GROUPS = 8
MOE_EXPERTS_PER_GROUP = 8
MOE_EXPERTS = MOE_GROUPS * MOE_EXPERTS_PER_GROUP
MOE_TOP_K = 2
ROUTER_LANES = 128

V7X_VMEM_LIMIT_BYTES = 60 * 1024 * 1024
ROW_TILE = 512
MM_ROW_TILE = 1024
COL_TILE = 512
MOE_ROW_BLOCK = 512
MOE_ROW_SIZES = (256, 320, 384, 448, 512)
MOE_FF_TILE = 256
CAST_ROWS = 256


def _params(*sem):
    return pltpu.CompilerParams(dimension_semantics=sem, vmem_limit_bytes=V7X_VMEM_LIMIT_BYTES)


def _sigmoid(x):
    return 1.0 / (1.0 + jnp.exp(-x))


def _cast_rows(src_ref, dst_ref):
    rows = src_ref.shape[0]
    step = min(CAST_ROWS, rows)

    def body(i, c):
        r = pl.multiple_of(i * step, step)
        dst_ref[pl.ds(r, step), :] = src_ref[pl.ds(r, step), :].astype(BF16)
        return c

    lax.fori_loop(0, rows // step, body, 0)


def _cast_cols_transposed(src_ref, dst_ref):
    k = src_ref.shape[1]
    step = min(CAST_ROWS, k)

    def body(i, c):
        r = pl.multiple_of(i * step, step)
        dst_ref[pl.ds(r, step), :] = src_ref[:, pl.ds(r, step)].T.astype(BF16)
        return c

    lax.fori_loop(0, k // step, body, 0)


def _rms(x, g):
    ms = jnp.mean(x * x, axis=-1, keepdims=True)
    return x * lax.rsqrt(ms + EPS) * g


def _norm_kernel(h_ref, g_ref, o_ref):
    o_ref[...] = _rms(h_ref[...], g_ref[...]).astype(o_ref.dtype)


def _norm_proj_kernel(h_ref, g_ref, w_ref, o_ref, s_ref):
    y = _rms(h_ref[...], g_ref[...]).astype(BF16)
    o_ref[...] = y
    s_ref[...] = lax.dot_general(y, w_ref[...].astype(BF16), _NT, preferred_element_type=F32)


def _route(logits):
    lane = lax.broadcasted_iota(jnp.int32, logits.shape, 1)
    neg = jnp.float32(-jnp.inf)
    is_g = lane < MOE_GROUPS
    lg = jnp.where(is_g, logits, neg)
    mg = jnp.max(lg, axis=-1, keepdims=True)
    g_sel = jnp.min(jnp.where(lg == mg, lane, ROUTER_LANES), axis=-1, keepdims=True)
    p_g = 1.0 / jnp.sum(jnp.where(is_g, jnp.exp(lg - mg), 0.0), axis=-1, keepdims=True)
    lo = MOE_GROUPS + g_sel * MOE_EXPERTS_PER_GROUP
    in_grp = (lane >= lo) & (lane < lo + MOE_EXPERTS_PER_GROUP)
    le = jnp.where(in_grp, logits, neg)
    v0 = jnp.max(le, axis=-1, keepdims=True)
    i0 = jnp.min(jnp.where(le == v0, lane, ROUTER_LANES), axis=-1, keepdims=True)
    le1 = jnp.where(lane == i0, neg, le)
    v1 = jnp.max(le1, axis=-1, keepdims=True)
    i1 = jnp.min(jnp.where(le1 == v1, lane, ROUTER_LANES), axis=-1, keepdims=True)
    e1 = jnp.exp(v1 - v0)
    g0 = p_g / (1.0 + e1)
    g1 = p_g * e1 / (1.0 + e1)
    return i0 - MOE_GROUPS, i1 - MOE_GROUPS, g0, g1


def _norm_router_kernel(h_ref, g_ref, w_ref, b_ref, o_ref, eid_ref, gate_ref):
    y = _rms(h_ref[...], g_ref[...])
    o_ref[...] = _pack_bf16_halves(y)
    logits = jnp.dot(y.astype(BF16), w_ref[...].astype(BF16), preferred_element_type=F32)
    e0, e1, g0, g1 = _route(logits + b_ref[...])
    eid_ref[:, 0:1] = e0
    eid_ref[:, 1:2] = e1
    gate_ref[:, 0:1] = g0
    gate_ref[:, 1:2] = g1


def _row_grid(t, tm):
    return (pl.cdiv(t, tm),)


def _rmsnorm(h, g, layer, out_dtype=BF16, rows=None, row_block0=0, tm=ROW_TILE):
    t, d = h.shape
    rows = t if rows is None else rows
    tm = min(tm, rows)
    return pl.pallas_call(
        _norm_kernel,
        grid=_row_grid(rows, tm),
        in_specs=[pl.BlockSpec((tm, d), lambda i: (row_block0 + i, 0)),
                  pl.BlockSpec((None, 1, d), lambda i: (layer, 0, 0))],
        out_specs=pl.BlockSpec((tm, d), lambda i: (i, 0)),
        out_shape=jax.ShapeDtypeStruct((rows, d), out_dtype),
        compiler_params=_params("parallel"),
    )(h, g)


def _rmsnorm_proj(h, g, layer, w_t, w_layer, col0, ns, tm=ROW_TILE):
    t, d = h.shape
    assert col0 % ns == 0 and ns % 8 == 0
    return pl.pallas_call(
        _norm_proj_kernel,
        grid=_row_grid(t, tm),
        in_specs=[pl.BlockSpec((tm, d), lambda i: (i, 0)),
                  pl.BlockSpec((None, 1, d), lambda i: (layer, 0, 0)),
                  pl.BlockSpec((None, ns, d), lambda i: (w_layer, col0 // ns, 0))],
        out_specs=[pl.BlockSpec((tm, d), lambda i: (i, 0)),
                   pl.BlockSpec((tm, ns), lambda i: (i, 0))],
        out_shape=[jax.ShapeDtypeStruct((t, d), BF16), jax.ShapeDtypeStruct((t, ns), F32)],
        compiler_params=_params("parallel"),
    )(h, g, w_t)


def _rmsnorm_router(h, g, layer, w_router, b_router, tm=ROW_TILE // 2):
    t, d = h.shape
    return pl.pallas_call(
        _norm_router_kernel,
        grid=_row_grid(t, tm),
        in_specs=[pl.BlockSpec((tm, d), lambda i: (i, 0)),
                  pl.BlockSpec((None, 1, d), lambda i: (layer, 0, 0)),
                  pl.BlockSpec((d, ROUTER_LANES), lambda i: (0, 0)),
                  pl.BlockSpec((1, ROUTER_LANES), lambda i: (0, 0))],
        out_specs=[pl.BlockSpec((tm, d // 2), lambda i: (i, 0)),
                   pl.BlockSpec((tm, MOE_TOP_K), lambda i: (i, 0)),
                   pl.BlockSpec((tm, MOE_TOP_K), lambda i: (i, 0))],
        out_shape=[jax.ShapeDtypeStruct((t, d // 2), jnp.uint32),
                   jax.ShapeDtypeStruct((t, MOE_TOP_K), jnp.int32),
                   jax.ShapeDtypeStruct((t, MOE_TOP_K), F32)],
        compiler_params=_params("parallel"),
    )(h, g, w_router, b_router)


def _mm_kernel(*refs, mode, tail, split_a, split_r, w_cols_major):
    a_ref, refs = refs[0], refs[1:]
    at_ref = a_ref
    if split_a:
        at_ref, refs = refs[0], refs[1:]
    w_ref, refs = refs[0], refs[1:]
    r_ref = rt_ref = None
    if mode != "plain":
        r_ref, refs = refs[0], refs[1:]
        rt_ref = r_ref
        if split_r:
            rt_ref, refs = refs[0], refs[1:]
    if mode == "ple":
        p_ref, wp_ref, o_ref, wb_ref, wpb_ref = refs
    else:
        o_ref, wb_ref = refs
    m = pl.program_id(1)

    @pl.when(m == 0)
    def _():
        if w_cols_major:
            _cast_cols_transposed(w_ref, wb_ref)
        else:
            _cast_rows(w_ref, wb_ref)
        if mode == "ple":
            _cast_rows(wp_ref, wpb_ref)

    def tile(src_ref, res_ref, rows):
        acc = jnp.dot(src_ref[rows, :], wb_ref[...], preferred_element_type=F32)
        if mode == "plain":
            o_ref[rows, :] = acc.astype(o_ref.dtype)
        elif mode == "res":
            o_ref[rows, :] = res_ref[rows, :] + acc
        else:
            pp = jnp.dot(p_ref[rows, :].astype(BF16), wpb_ref[...], preferred_element_type=F32)
            o_ref[rows, :] = res_ref[rows, :] + pp * _sigmoid(acc)

    if tail == 0:
        tile(a_ref, r_ref, slice(None))
    else:
        last = pl.num_programs(1) - 1
        pl.when(m < last)(lambda: tile(a_ref, r_ref, slice(None)))
        pl.when(m == last)(lambda: tile(at_ref, rt_ref, slice(0, tail)))


def _matmul(a, w, layer, n_cols, mode="plain", res=None, p=None, wp=None, a_tail=None, res_tail=None,
            w_cols_major=False, out_dtype=F32, tm=MM_ROW_TILE, tn=COL_TILE):
    k = a.shape[1]
    tm = min(tm, a.shape[0] // 256 * 256)
    split_a = a_tail is not None
    split_r = res_tail is not None
    if split_a:
        assert a.shape[0] % tm == 0 and a_tail.shape[0] < tm
        t, tail = a.shape[0] + a_tail.shape[0], a_tail.shape[0]
    else:
        t, tail = a.shape[0], a.shape[0] % tm
    if split_r:
        assert res.shape[0] == t - tail and res_tail.shape[0] == tail
    n_full = t // tm

    def full_rows(n, m):
        return jnp.minimum(m, n_full - 1)

    tn = min(tn, n_cols)
    assert n_cols % tn == 0
    grid = (n_cols // tn, pl.cdiv(t, tm))
    if split_a:
        in_specs = [pl.BlockSpec((tm, k), lambda n, m: (full_rows(n, m), 0)),
                    pl.BlockSpec((tail, k), lambda n, m: (0, 0))]
        args = [a, a_tail]
    else:
        in_specs = [pl.BlockSpec((tm, k), lambda n, m: (m, 0))]
        args = [a]
    if w_cols_major:
        in_specs.append(pl.BlockSpec((None, tn, k), lambda n, m: (layer, n, 0)))
    else:
        in_specs.append(pl.BlockSpec((None, k, tn), lambda n, m: (layer, 0, n)))
    scratch = [pltpu.VMEM((k, tn), BF16)]
    args.append(w)
    if mode in ("res", "ple"):
        if split_r:
            in_specs += [pl.BlockSpec((tm, tn), lambda n, m: (full_rows(n, m), n)),
                         pl.BlockSpec((tail, tn), lambda n, m: (0, n))]
            args += [res, res_tail]
        else:
            in_specs.append(pl.BlockSpec((tm, tn), lambda n, m: (m, n)))
            args.append(res)
    if mode == "ple":
        kp = p.shape[-1]
        in_specs += [pl.BlockSpec((None, tm, kp), lambda n, m: (layer, m, 0)),
                     pl.BlockSpec((None, kp, tn), lambda n, m: (layer, 0, n))]
        args += [p, wp]
        scratch.append(pltpu.VMEM((kp, tn), BF16))
    return pl.pallas_call(
        functools.partial(_mm_kernel, mode=mode, tail=tail, split_a=split_a, split_r=split_r,
                          w_cols_major=w_cols_major),
        grid=grid,
        in_specs=in_specs,
        out_specs=pl.BlockSpec((tm, tn), lambda n, m: (m, n)),
        out_shape=jax.ShapeDtypeStruct((t, n_cols), out_dtype),
        scratch_shapes=scratch,
        compiler_params=_params("arbitrary", "arbitrary"),
    )(*args)


def _softmax_sink_pv(s, sk, vb):
    m = jnp.maximum(jnp.max(s, axis=-1, keepdims=True), sk)
    p = jnp.exp(s - m)
    den = jnp.sum(p, axis=-1, keepdims=True) + jnp.exp(sk - m)
    return jnp.dot((p / den).astype(BF16), vb, preferred_element_type=F32)


_NT = (((1,), (1,)), ((), ()))


def _swa_prompt_kernel(sinks_ref, q_ref, kp_ref, kc_ref, vp_ref, vc_ref, o_ref, nk_ref, nv_ref, *, kvh, group):
    n = pl.program_id(1)
    w, kvd = kc_ref.shape
    hd = kvd // kvh
    kk = jnp.concatenate([kp_ref[...], kc_ref[...]], axis=0).astype(BF16)
    vv = jnp.concatenate([vp_ref[...], vc_ref[...]], axis=0).astype(BF16)
    qi = lax.broadcasted_iota(jnp.int32, (w, 2 * w), 0)
    kj = lax.broadcasted_iota(jnp.int32, (w, 2 * w), 1)
    mask = (kj > qi) & (kj <= qi + w) & (kj >= jnp.where(n > 0, 0, w))
    scale = hd ** -0.5
    for kv in range(kvh):
        kh = kk[:, kv * hd:(kv + 1) * hd]
        vh = vv[:, kv * hd:(kv + 1) * hd]
        for g in range(group):
            c0 = (kv * group + g) * hd
            qg = (q_ref[:, c0:c0 + hd] * scale).astype(BF16)
            s = lax.dot_general(qg, kh, _NT, preferred_element_type=F32)
            s = jnp.where(mask, s, -jnp.inf)
            o = _softmax_sink_pv(s, sinks_ref[kv * group + g], vh)
            o_ref[:, c0:c0 + hd] = o.astype(o_ref.dtype)

    @pl.when(n == pl.num_programs(1) - 1)
    def _():
        nk_ref[...] = kc_ref[...]
        nv_ref[...] = vc_ref[...]


def _swa_prompt(qkv, sinks, batch, seq, kvh):
    hd, w = SWA_HEAD_DIM, WINDOW
    kvd = kvh * hd
    qd = qkv.shape[1] - 2 * kvd
    group = qd // kvd
    nb = seq // w
    kcol = qd // kvd

    def cur(b, n):
        return b * nb + n

    def prev(b, n):
        return b * nb + jnp.maximum(n - 1, 0)

    last_spec = pl.BlockSpec((None, None, w, kvd), lambda b, n: (0, b, 0, 0))
    return pl.pallas_call(
        functools.partial(_swa_prompt_kernel, kvh=kvh, group=group),
        grid=(batch, nb),
        in_specs=[pl.BlockSpec(memory_space=pltpu.SMEM),
                  pl.BlockSpec((w, qd), lambda b, n: (cur(b, n), 0)),
                  pl.BlockSpec((w, kvd), lambda b, n: (prev(b, n), kcol)),
                  pl.BlockSpec((w, kvd), lambda b, n: (cur(b, n), kcol)),
                  pl.BlockSpec((w, kvd), lambda b, n: (prev(b, n), kcol + 1)),
                  pl.BlockSpec((w, kvd), lambda b, n: (cur(b, n), kcol + 1))],
        out_specs=[pl.BlockSpec((w, qd), lambda b, n: (cur(b, n), 0)), last_spec, last_spec],
        out_shape=[jax.ShapeDtypeStruct((batch * seq, qd), BF16),
                   jax.ShapeDtypeStruct((1, batch, w, kvd), F32),
                   jax.ShapeDtypeStruct((1, batch, w, kvd), F32)],
        compiler_params=_params("parallel", "arbitrary"),
    )(sinks, qkv, qkv, qkv, qkv, qkv)


def _swa_decode_kernel(sk_ref, q_ref, x_ref, kp_ref, vp_ref, nk_ref, nv_ref, o_ref, *, kvh, group):
    bb, w, kvd = kp_ref.shape
    hd = kvd // kvh
    nh = kvh * group
    qd = nh * hd
    scale = hd ** -0.5
    row = lax.broadcasted_iota(jnp.int32, (w, kvd), 0)
    head_kv = lax.broadcasted_iota(jnp.int32, (nh, hd), 0) // group
    sk = sk_ref[...]
    for bi in range(bb):
        x = x_ref[bi:bi + 1, :]
        kc = jnp.where(row == w - 1, x[:, qd:qd + kvd], pltpu.roll(kp_ref[bi], w - 1, axis=0))
        vc = jnp.where(row == w - 1, x[:, qd + kvd:], pltpu.roll(vp_ref[bi], w - 1, axis=0))
        nk_ref[bi] = kc
        nv_ref[bi] = vc
        s = lax.dot_general((q_ref[bi] * scale).astype(BF16), kc.astype(BF16), _NT, preferred_element_type=F32)
        ov = _softmax_sink_pv(s, sk, vc.astype(BF16))
        o = jnp.zeros((nh, hd), F32)
        for kv in range(kvh):
            o = jnp.where(head_kv == kv, ov[:, kv * hd:(kv + 1) * hd], o)
        o_ref[bi] = o


def _swa_decode(qkv, sinks, k_past, v_past, layer, row0, bb=8):
    nl, b, w, kvh, hd = k_past.shape
    kvd = kvh * hd
    qd = qkv.shape[1] - 2 * kvd
    group = qd // kvd
    nh = kvh * group
    bb = min(bb, b)
    blk0 = row0 // bb
    q = qkv[row0:row0 + b, :qd].reshape(b, nh, 1, hd)
    own = (jnp.arange(nh)[:, None] // group == jnp.arange(kvh)[None, :])[None, :, :, None]
    q_blocks = jnp.where(own, q, 0.0).reshape(b, nh, kvd)
    in_spec = pl.BlockSpec((None, bb, w, kvd), lambda i: (layer, i, 0, 0))
    out_spec = pl.BlockSpec((None, bb, w, kvd), lambda i: (0, i, 0, 0))
    nk, nv, o = pl.pallas_call(
        functools.partial(_swa_decode_kernel, kvh=kvh, group=group),
        grid=(b // bb,),
        in_specs=[pl.BlockSpec((nh, 1), lambda i: (0, 0)),
                  pl.BlockSpec((bb, nh, kvd), lambda i: (i, 0, 0)),
                  pl.BlockSpec((bb, qkv.shape[1]), lambda i: (blk0 + i, 0)),
                  in_spec, in_spec],
        out_specs=[out_spec, out_spec, pl.BlockSpec((bb, nh, hd), lambda i: (i, 0, 0))],
        out_shape=[jax.ShapeDtypeStruct((1, b, w, kvd), F32), jax.ShapeDtypeStruct((1, b, w, kvd), F32),
                   jax.ShapeDtypeStruct((b, nh, hd), F32)],
        compiler_params=_params("parallel"),
    )(sinks.reshape(nh, 1), q_blocks, qkv, k_past.reshape(nl, b, w, kvd), v_past.reshape(nl, b, w, kvd))
    return o.reshape(b, qd), nk.reshape(1, b, w, kvh, hd), nv.reshape(1, b, w, kvh, hd)


def _log_decay(gl, wgu, bg):
    x = jnp.dot(gl.astype(BF16), wgu.astype(BF16), preferred_element_type=F32) + bg
    return (jnp.minimum(x, 0.0) - jnp.log(1.0 + jnp.exp(-jnp.abs(x)))) * (1.0 / GLA_TAU)


def _gla_out(o, ng, r):
    return _rms(o, ng) * (r * _sigmoid(r))


def _block_ref_row(x, row, s):
    c, d = x.shape
    if 2 * s >= 8:
        nblk = c // (2 * s)
        ref = x.reshape(nblk, 2 * s, d)[:, s - 1:s, :]
        return jnp.broadcast_to(ref, (nblk, 2 * s, d)).reshape(c, d)
    pos = row & (2 * s - 1)
    out = x
    for delta in range(s - 1, -s - 1, -1):
        if delta != 0:
            out = jnp.where(pos == s - 1 - delta, pltpu.roll(x, (c - delta) % c, axis=0), out)
    return out


def _gla_prompt_kernel(q_ref, k_ref, v_ref, r_ref, gl_ref, wgu_ref, bg_ref, ng_ref, o_ref, s_ref):
    c, dkh = q_ref.shape

    @pl.when(pl.program_id(2) == 0)
    def _():
        s_ref[...] = jnp.zeros_like(s_ref)

    g = _log_decay(gl_ref[...], wgu_ref[...], bg_ref[...])
    row = lax.broadcasted_iota(jnp.int32, (c, dkh), 0)
    cum = g
    sh = 1
    while sh < c:
        cum = cum + jnp.where(row >= sh, pltpu.roll(cum, sh, axis=0), 0.0)
        sh *= 2
    q = q_ref[...] * (dkh ** -0.5)
    k = k_ref[...]
    vb = v_ref[...].astype(BF16)

    ri = lax.broadcasted_iota(jnp.int32, (c, c), 0)
    ci = lax.broadcasted_iota(jnp.int32, (c, c), 1)
    att = jnp.where(ri == ci, lax.dot_general(q.astype(BF16), k.astype(BF16), _NT, preferred_element_type=F32), 0.0)
    s = c // 2
    while s >= 1:
        cref = _block_ref_row(cum, row, s)
        e = jnp.exp(jnp.where((row & s) != 0, cum - cref, cref - cum))
        a_s = lax.dot_general((q * e).astype(BF16), (k * e).astype(BF16), _NT, preferred_element_type=F32)
        pair = (((ri ^ ci) >> (s.bit_length() - 1)) == 1) & ((ri & s) != 0)
        att = att + jnp.where(pair, a_s, 0.0)
        s //= 2

    st = s_ref[...]
    o = jnp.dot((q * jnp.exp(cum)).astype(BF16), st.astype(BF16), preferred_element_type=F32)
    o = o + jnp.dot(att.astype(BF16), vb, preferred_element_type=F32)
    o_ref[...] = _gla_out(o, ng_ref[...], r_ref[...]).astype(o_ref.dtype)

    last = cum[c - 1:c, :]
    kd_t = (k * jnp.exp(last - cum)).T.astype(BF16)
    e_col = jnp.exp(cum.T[:, c - 1:c])
    s_ref[...] = e_col * st + jnp.dot(kd_t, vb, preferred_element_type=F32)


def _gla_prompt(proj, gl, w_gate_up, b_gate, norm_g, batch, seq, t_out):
    h = GLA_HEADS
    dk = w_gate_up.shape[-1]
    dkh = dk // h
    dv = (proj.shape[1] - 2 * dk) // 2
    dvh = dv // h
    c = GLA_CHUNK
    nc = seq // c

    def rows(b, hh, i):
        return b * nc + i

    return pl.pallas_call(
        _gla_prompt_kernel,
        grid=(batch, h, nc),
        in_specs=[pl.BlockSpec((c, dkh), lambda b, hh, i: (rows(b, hh, i), hh)),
                  pl.BlockSpec((c, dkh), lambda b, hh, i: (rows(b, hh, i), h + hh)),
                  pl.BlockSpec((c, dvh), lambda b, hh, i: (rows(b, hh, i), 2 * dk // dvh + hh)),
                  pl.BlockSpec((c, dvh), lambda b, hh, i: (rows(b, hh, i), (2 * dk + dv) // dvh + hh)),
                  pl.BlockSpec((c, GLA_GATE_RANK), lambda b, hh, i: (rows(b, hh, i), 0)),
                  pl.BlockSpec((None, GLA_GATE_RANK, dkh), lambda b, hh, i: (0, 0, hh)),
                  pl.BlockSpec((1, dkh), lambda b, hh, i: (0, hh)),
                  pl.BlockSpec((1, dvh), lambda b, hh, i: (0, 0))],
        out_specs=[pl.BlockSpec((c, dvh), lambda b, hh, i: (rows(b, hh, i), hh)),
                   pl.BlockSpec((None, None, None, dkh, dvh), lambda b, hh, i: (0, b, hh, 0, 0))],
        out_shape=[jax.ShapeDtypeStruct((t_out, dv), BF16),
                   jax.ShapeDtypeStruct((1, batch, h, dkh, dvh), F32)],
        compiler_params=_params("parallel", "parallel", "arbitrary"),
    )(proj, proj, proj, proj, gl, w_gate_up, b_gate, norm_g)


def _gla_gate_kernel(gl_ref, wgu_ref, bg_ref, o_ref):
    o_ref[...] = _log_decay(gl_ref[...], wgu_ref[...], bg_ref[...])


def _gla_gate(gl, w_gate_up, b_gate, row0, rows):
    dk = w_gate_up.shape[-1]
    return pl.pallas_call(
        _gla_gate_kernel,
        grid=(1,),
        in_specs=[pl.BlockSpec((rows, GLA_GATE_RANK), lambda i: (row0 // rows, 0)),
                  pl.BlockSpec((None, GLA_GATE_RANK, dk), lambda i: (0, 0, 0)),
                  pl.BlockSpec((1, dk), lambda i: (0, 0))],
        out_specs=pl.BlockSpec((rows, dk), lambda i: (0, 0)),
        out_shape=jax.ShapeDtypeStruct((rows, dk), F32),
        compiler_params=_params("arbitrary"),
    )(gl, w_gate_up, b_gate)


def _gla_decode_kernel(q_ref, k_ref, g_ref, v_ref, r_ref, ng_ref, s_ref, o_ref, ns_ref):
    dkh = q_ref.shape[-1]
    rows = jnp.concatenate([q_ref[...], k_ref[...], g_ref[...], jnp.zeros((128 - 3, dkh), F32)], axis=0)
    cols = rows.T
    q_col, k_col, g_col = cols[:, 0:1], cols[:, 1:2], cols[:, 2:3]
    sn = jnp.exp(g_col) * s_ref[...] + k_col * v_ref[...]
    ns_ref[...] = sn
    o = jnp.sum((q_col * (dkh ** -0.5)) * sn, axis=0, keepdims=True)
    o_ref[...] = _gla_out(o, ng_ref[...], r_ref[...])


def _gla_decode(proj_s, log_a, norm_g, state, layer):
    b, h, dkh, dvh = state.shape[1:]
    dk, dv = h * dkh, h * dvh
    st_spec = pl.BlockSpec((None, None, None, dkh, dvh), lambda i, hh: (layer, i, hh, 0, 0))
    st_out_spec = pl.BlockSpec((None, None, None, dkh, dvh), lambda i, hh: (0, i, hh, 0, 0))
    return pl.pallas_call(
        _gla_decode_kernel,
        grid=(b, h),
        in_specs=[pl.BlockSpec((None, 1, dkh), lambda i, hh: (i, 0, hh)),
                  pl.BlockSpec((None, 1, dkh), lambda i, hh: (i, 0, h + hh)),
                  pl.BlockSpec((None, 1, dkh), lambda i, hh: (i, 0, hh)),
                  pl.BlockSpec((None, 1, dvh), lambda i, hh: (i, 0, 2 * dk // dvh + hh)),
                  pl.BlockSpec((None, 1, dvh), lambda i, hh: (i, 0, (2 * dk + dv) // dvh + hh)),
                  pl.BlockSpec((1, dvh), lambda i, hh: (0, 0)),
                  st_spec],
        out_specs=[pl.BlockSpec((None, 1, dvh), lambda i, hh: (i, 0, hh)), st_out_spec],
        out_shape=[jax.ShapeDtypeStruct((b, 1, dv), F32), jax.ShapeDtypeStruct((1, b, h, dkh, dvh), F32)],
        compiler_params=_params("parallel", "parallel"),
    )(proj_s, proj_s, log_a.reshape(b, 1, dk), proj_s, proj_s, norm_g, state)


def _pack_bf16_halves(y):
    half = y.shape[1] // 2
    bits = lax.bitcast_convert_type(y.astype(BF16).astype(F32), jnp.uint32)
    return (bits[:, :half] >> 16) | (bits[:, half:] & jnp.uint32(0xFFFF0000))


def _unpack_bf16_halves(u):
    lo = lax.bitcast_convert_type(u << 16, F32)
    hi = lax.bitcast_convert_type(u & jnp.uint32(0xFFFF0000), F32)
    return jnp.concatenate([lo, hi], axis=1).astype(BF16)


def _for_rows(n, fn, unroll=8):
    def group(j, c):
        for u in range(unroll):
            fn(j * unroll + u)
        return c

    def single(r, c):
        fn(r)
        return c

    full = n // unroll
    lax.fori_loop(0, full, group, 0)
    lax.fori_loop(full * unroll, n, single, 0)


def _moe_kernel(ie_ref, ir_ref, in_ref, tok_ref, dst_ref, x_hbm, wg_ref, wu_ref, wd_ref, y_hbm,
                xbuf, ybuf, gsem, ssem):
    i = pl.program_id(0)
    f = pl.program_id(1)
    ni = pl.num_programs(0)
    nf = pl.num_programs(1)
    n = in_ref[i]
    slot = i % 2
    rb = ybuf.shape[0]
    sizes = sorted({min(m, rb) for m in MOE_ROW_SIZES})

    def gather(item, sl, r):
        src = x_hbm.at[pl.ds(tok_ref[ir_ref[item] + r], 1)]
        return pltpu.make_async_copy(src, xbuf.at[sl, pl.ds(r, 1)], gsem.at[sl])

    def scatter(item, r):
        dst = y_hbm.at[pl.ds(dst_ref[ir_ref[item] + r], 1)]
        return pltpu.make_async_copy(ybuf.at[pl.ds(r, 1)], dst, ssem)

    @pl.when(f == 0)
    def _():
        @pl.when(i == 0)
        def _():
            xbuf[...] = jnp.zeros_like(xbuf)
            ybuf[...] = jnp.zeros_like(ybuf)
            _for_rows(n, lambda r: gather(0, 0, r).start())

        nxt = jnp.minimum(i + 1, ni - 1)
        n_nxt = jnp.where(i + 1 < ni, in_ref[nxt], 0)
        _for_rows(n_nxt, lambda r: gather(nxt, 1 - slot, r).start())
        _for_rows(n, lambda r: gather(i, slot, r).wait())
        prv = jnp.maximum(i - 1, 0)
        n_prv = jnp.where(i > 0, in_ref[prv], 0)
        _for_rows(n_prv, lambda r: scatter(prv, r).wait())

    lo = 0
    for m in sizes:
        @pl.when((n > lo) & (n <= m))
        def _(m=m):
            xs = _unpack_bf16_halves(xbuf[slot, :m, :])
            hg = jnp.dot(xs, wg_ref[...].astype(BF16), preferred_element_type=F32)
            hu = jnp.dot(xs, wu_ref[...].astype(BF16), preferred_element_type=F32)
            act = (hg * _sigmoid(hg) * hu).astype(BF16)
            y = jnp.dot(act, wd_ref[...].astype(BF16), preferred_element_type=F32)
            ybuf[:m, :] = y + jnp.where(f > 0, ybuf[:m, :], 0.0)
        lo = m

    @pl.when(f == nf - 1)
    def _():
        _for_rows(n, lambda r: scatter(i, r).start())

        @pl.when(i == ni - 1)
        def _():
            _for_rows(n, lambda r: scatter(i, r).wait())


def _moe_experts(xn, eid, w_gate, w_up, w_down, layer):
    t, d = xn.shape[0], xn.shape[1] * 2
    n_exp, _, dff = w_gate.shape[1:]
    a = t * MOE_TOP_K
    rb = min(MOE_ROW_BLOCK, -(-a // 8) * 8)
    fc = min(MOE_FF_TILE, dff)
    nf = dff // fc
    n_items = n_exp + a // rb

    flat = eid.reshape(-1)
    order = jnp.argsort(flat, stable=True).astype(jnp.int32)
    counts = jnp.sum((flat[:, None] == jnp.arange(n_exp, dtype=jnp.int32)[None, :]).astype(jnp.int32), axis=0)
    starts = jnp.cumsum(counts) - counts
    per_e = (counts + rb - 1) // rb
    item_end = jnp.cumsum(per_e)
    item_start = item_end - per_e
    ii = jnp.arange(n_items, dtype=jnp.int32)
    valid = ii < item_end[-1]
    e_of = jnp.minimum(jnp.searchsorted(item_end, ii, side="right").astype(jnp.int32), n_exp - 1)
    local = ii - item_start[e_of]
    item_rows = jnp.where(valid, jnp.clip(counts[e_of] - local * rb, 0, rb), 0).astype(jnp.int32)
    item_row0 = jnp.where(valid, starts[e_of] + local * rb, 0).astype(jnp.int32)
    e_last = jnp.max(jnp.where(valid, e_of, 0))
    item_e = jnp.where(valid, e_of, e_last).astype(jnp.int32)
    tok = order // MOE_TOP_K
    dst = (order % MOE_TOP_K) * t + tok

    def w_in_map(i, f, ie, ir, inn, tk, ds_):
        return (layer, ie[i], 0, jnp.where(inn[i] > 0, f, nf - 1))

    def w_out_map(i, f, ie, ir, inn, tk, ds_):
        return (layer, ie[i], jnp.where(inn[i] > 0, f, nf - 1), 0)

    grid_spec = pltpu.PrefetchScalarGridSpec(
        num_scalar_prefetch=5,
        grid=(n_items, nf),
        in_specs=[pl.BlockSpec(memory_space=pl.ANY),
                  pl.BlockSpec((None, None, d, fc), w_in_map),
                  pl.BlockSpec((None, None, d, fc), w_in_map),
                  pl.BlockSpec((None, None, fc, d), w_out_map)],
        out_specs=pl.BlockSpec(memory_space=pl.ANY),
        scratch_shapes=[pltpu.VMEM((2, rb, d // 2), jnp.uint32), pltpu.VMEM((rb, d), F32),
                        pltpu.SemaphoreType.DMA((2,)), pltpu.SemaphoreType.DMA(())],
    )
    y = pl.pallas_call(
        _moe_kernel,
        grid_spec=grid_spec,
        out_shape=jax.ShapeDtypeStruct((a, d), F32),
        compiler_params=_params("arbitrary", "arbitrary"),
    )(item_e, item_row0, item_rows, tok, dst, xn, w_gate, w_up, w_down)
    return y.reshape(MOE_TOP_K, t, d)


def _combine_norm_kernel(h_ref, y0_ref, y1_ref, gate_ref, g_ref, ho_ref, a_ref):
    h = h_ref[...] + (y0_ref[...] * gate_ref[:, 0:1] + y1_ref[...] * gate_ref[:, 1:2])
    ho_ref[...] = h
    a_ref[...] = _rms(h, g_ref[...]).astype(a_ref.dtype)


def _combine_norm(h, y2, gates, g, layer, tm=ROW_TILE // 2):
    t, d = h.shape
    return pl.pallas_call(
        _combine_norm_kernel,
        grid=_row_grid(t, tm),
        in_specs=[pl.BlockSpec((tm, d), lambda i: (i, 0)),
                  pl.BlockSpec((None, tm, d), lambda i: (0, i, 0)),
                  pl.BlockSpec((None, tm, d), lambda i: (1, i, 0)),
                  pl.BlockSpec((tm, MOE_TOP_K), lambda i: (i, 0)),
                  pl.BlockSpec((None, 1, d), lambda i: (layer, 0, 0))],
        out_specs=[pl.BlockSpec((tm, d), lambda i: (i, 0)), pl.BlockSpec((tm, d), lambda i: (i, 0))],
        out_shape=[jax.ShapeDtypeStruct((t, d), F32), jax.ShapeDtypeStruct((t, d), BF16)],
        compiler_params=_params("parallel"),
    )(h, y2, y2, gates, g)


def kernel(x_prompt, x_sample, cache_swa_k, cache_swa_v, state_gla, p_prompt, p_sample, ln_mix, ln_ffn, ln_ple, ln_final, swa_w_in, swa_sinks, swa_w_out, gla_w_in, gla_w_gate_up, gla_b_gate, gla_norm, gla_w_out, moe_w_group, moe_b_group, moe_w_expert, moe_b_expert, moe_w_gate, moe_w_up, moe_w_down, ple_w_proj, ple_w_gate):
    batch, seq, d = x_prompt.shape
    dec = x_sample.shape[0]
    assert x_sample.shape[1] == 1
    tp = batch * seq
    t = tp + dec
    depth = ln_mix.shape[0]
    kvh = cache_swa_k.shape[3]
    kvd = kvh * SWA_HEAD_DIM
    dk = gla_w_gate_up.shape[-1]
    dv = (gla_w_in.shape[-1] - GLA_GATE_RANK - 2 * dk) // 2

    x_p, x_s = x_prompt.reshape(tp, d), x_sample.reshape(dec, d)
    h = None
    p_all = jnp.concatenate([p_prompt.reshape(depth, tp, -1), p_sample.reshape(depth, dec, -1)], axis=1)
    pad = ROUTER_LANES - MOE_GROUPS - MOE_EXPERTS
    w_router = jnp.concatenate([moe_w_group, moe_w_expert, jnp.zeros((depth, d, pad), F32)], axis=-1)
    b_router = jnp.concatenate([moe_b_group, moe_b_expert, jnp.zeros((depth, pad), F32)], axis=-1)
    ln_mix, ln_ffn, ln_ple = (g.reshape(depth, 1, d) for g in (ln_mix, ln_ffn, ln_ple))

    new_k_p, new_v_p, new_s_p, new_k_s, new_v_s, new_s_s = [], [], [], [], [], []
    for i in range(depth):
        j = i // 2
        if i % 2 == 0:
            if h is None:
                qkv = _matmul(_rmsnorm(x_p, ln_mix, i), swa_w_in, j, swa_w_in.shape[-1],
                              a_tail=_rmsnorm(x_s, ln_mix, i))
            else:
                qkv = _matmul(_rmsnorm(h, ln_mix, i), swa_w_in, j, swa_w_in.shape[-1])
            o, nk_p, nv_p = _swa_prompt(qkv, swa_sinks[j], batch, seq, kvh)
            o_s, nk, nv = _swa_decode(qkv, swa_sinks[j], cache_swa_k, cache_swa_v, j, tp)
            new_k_p.append(nk_p.reshape(1, batch, WINDOW, kvh, SWA_HEAD_DIM))
            new_v_p.append(nv_p.reshape(1, batch, WINDOW, kvh, SWA_HEAD_DIM))
            new_k_s.append(nk)
            new_v_s.append(nv)
            if h is None:
                h = _matmul(o, swa_w_out, j, d, mode="res", res=x_p, res_tail=x_s, a_tail=o_s.astype(BF16))
            else:
                h = _matmul(o, swa_w_out, j, d, mode="res", res=h, a_tail=o_s.astype(BF16))
        else:
            w_in_t = jnp.swapaxes(gla_w_in, 1, 2)
            a, gl = _rmsnorm_proj(h, ln_mix, i, w_in_t, j, 2 * dk + 2 * dv, GLA_GATE_RANK)
            proj = _matmul(a, w_in_t, j, 2 * dk + 2 * dv, w_cols_major=True)
            o, s_p = _gla_prompt(proj, gl, gla_w_gate_up[j:j + 1], gla_b_gate[j:j + 1], gla_norm[j:j + 1], batch, seq, tp)
            log_a = _gla_gate(gl, gla_w_gate_up[j:j + 1], gla_b_gate[j:j + 1], tp, dec)
            o_s, s_s = _gla_decode(proj[tp:].reshape(dec, 1, -1), log_a, gla_norm[j:j + 1], state_gla, j)
            new_s_p.append(s_p)
            new_s_s.append(s_s)
            h = _matmul(o, gla_w_out, j, d, mode="res", res=h, a_tail=o_s.reshape(dec, dv).astype(BF16))
        xn, eid, gates = _rmsnorm_router(h, ln_ffn, i, w_router[i], b_router[i:i + 1])
        y2 = _moe_experts(xn, eid, moe_w_gate, moe_w_up, moe_w_down, i)
        h, a3 = _combine_norm(h, y2, gates, ln_ple, i)
        h = _matmul(a3, ple_w_gate, i, d, mode="ple", res=h, p=p_all, wp=ple_w_proj)

    g_fin = ln_final.reshape(1, 1, d)
    y_p = _rmsnorm(h, g_fin, 0, out_dtype=F32, rows=tp)
    y_s = _rmsnorm(h, g_fin, 0, out_dtype=F32, rows=dec, row_block0=tp // dec, tm=dec)
    return (y_p.reshape(batch, seq, d), y_s.reshape(dec, 1, d),
            *(per_layer[0] if len(per_layer) == 1 else jnp.concatenate(per_layer)
              for per_layer in (new_k_p, new_v_p, new_s_p, new_k_s, new_v_s, new_s_s)))
```

```python
import functools

import jax
import jax.numpy as jnp
from jax import lax
from jax.experimental import pallas as pl
from jax.experimental.pallas import tpu as pltpu

F32 = jnp.float32
BF16 = jnp.bfloat16

EPS = 1e-6
WINDOW = 128
SWA_HEAD_DIM = 128
GLA_HEADS = 4
GLA_GATE_RANK = 16
GLA_TAU = 16.0
GLA_CHUNK = 256
MOE_GROUPS = 8
MOE_EXPERTS_PER_GROUP = 8
MOE_EXPERTS = MOE_GROUPS * MOE_EXPERTS_PER_GROUP
MOE_TOP_K = 2
ROUTER_LANES = 128

V7X_VMEM_LIMIT_BYTES = 60 * 1024 * 1024
ROW_TILE = 512
MM_ROW_TILE = 1024
COL_TILE = 512
MOE_ROW_BLOCK = 512
MOE_ROW_SIZES = (256, 384, 512)
MOE_FF_TILE = 256
CAST_ROWS = 256


def _params(*sem):
    return pltpu.CompilerParams(dimension_semantics=sem, vmem_limit_bytes=V7X_VMEM_LIMIT_BYTES)


def _sigmoid(x):
    return 1.0 / (1.0 + jnp.exp(-x))


def _cast_rows(src_ref, dst_ref):
    rows = src_ref.shape[0]
    step = min(CAST_ROWS, rows)

    def body(i, c):
        r = pl.multiple_of(i * step, step)
        dst_ref[pl.ds(r, step), :] = src_ref[pl.ds(r, step), :].astype(BF16)
        return c

    lax.fori_loop(0, rows // step, body, 0)


def _cast_cols_transposed(src_ref, dst_ref):
    k = src_ref.shape[1]
    step = min(CAST_ROWS, k)

    def body(i, c):
        r = pl.multiple_of(i * step, step)
        dst_ref[pl.ds(r, step), :] = src_ref[:, pl.ds(r, step)].T.astype(BF16)
        return c

    lax.fori_loop(0, k // step, body, 0)


def _rms(x, g):
    ms = jnp.mean(x * x, axis=-1, keepdims=True)
    return x * lax.rsqrt(ms + EPS) * g


def _norm_kernel(h_ref, g_ref, o_ref):
    o_ref[...] = _rms(h_ref[...], g_ref[...]).astype(o_ref.dtype)


def _norm_proj_kernel(h_ref, g_ref, w_ref, o_ref, s_ref):
    y = _rms(h_ref[...], g_ref[...]).astype(BF16)
    o_ref[...] = y
    s_ref[...] = lax.dot_general(y, w_ref[...].astype(BF16), _NT, preferred_element_type=F32)


def _route(logits):
    lane = lax.broadcasted_iota(jnp.int32, logits.shape, 1)
    neg = jnp.float32(-jnp.inf)
    is_g = lane < MOE_GROUPS
    lg = jnp.where(is_g, logits, neg)
    mg = jnp.max(lg, axis=-1, keepdims=True)
    g_sel = jnp.min(jnp.where(lg == mg, lane, ROUTER_LANES), axis=-1, keepdims=True)
    p_g = 1.0 / jnp.sum(jnp.where(is_g, jnp.exp(lg - mg), 0.0), axis=-1, keepdims=True)
    lo = MOE_GROUPS + g_sel * MOE_EXPERTS_PER_GROUP
    in_grp = (lane >= lo) & (lane < lo + MOE_EXPERTS_PER_GROUP)
    le = jnp.where(in_grp, logits, neg)
    v0 = jnp.max(le, axis=-1, keepdims=True)
    i0 = jnp.min(jnp.where(le == v0, lane, ROUTER_LANES), axis=-1, keepdims=True)
    le1 = jnp.where(lane == i0, neg, le)
    v1 = jnp.max(le1, axis=-1, keepdims=True)
    i1 = jnp.min(jnp.where(le1 == v1, lane, ROUTER_LANES), axis=-1, keepdims=True)
    e1 = jnp.exp(v1 - v0)
    g0 = p_g / (1.0 + e1)
    g1 = p_g * e1 / (1.0 + e1)
    return i0 - MOE_GROUPS, i1 - MOE_GROUPS, g0, g1


def _norm_router_kernel(h_ref, g_ref, w_ref, b_ref, o_ref, eid_ref, gate_ref):
    y = _rms(h_ref[...], g_ref[...])
    o_ref[...] = _pack_bf16_halves(y)
    logits = jnp.dot(y.astype(BF16), w_ref[...].astype(BF16), preferred_element_type=F32)
    e0, e1, g0, g1 = _route(logits + b_ref[...])
    eid_ref[:, 0:1] = e0
    eid_ref[:, 1:2] = e1
    gate_ref[:, 0:1] = g0
    gate_ref[:, 1:2] = g1


def _row_grid(t, tm):
    return (pl.cdiv(t, tm),)


def _rmsnorm(h, g, layer, out_dtype=BF16, rows=None, row_block0=0, tm=ROW_TILE):
    t, d = h.shape
    rows = t if rows is None else rows
    tm = min(tm, rows)
    return pl.pallas_call(
        _norm_kernel,
        grid=_row_grid(rows, tm),
        in_specs=[pl.BlockSpec((tm, d), lambda i: (row_block0 + i, 0)),
                  pl.BlockSpec((None, 1, d), lambda i: (layer, 0, 0))],
        out_specs=pl.BlockSpec((tm, d), lambda i: (i, 0)),
        out_shape=jax.ShapeDtypeStruct((rows, d), out_dtype),
        compiler_params=_params("parallel"),
    )(h, g)


def _rmsnorm_proj(h, g, layer, w_t, w_layer, col0, ns, tm=ROW_TILE):
    t, d = h.shape
    assert col0 % ns == 0 and ns % 8 == 0
    return pl.pallas_call(
        _norm_proj_kernel,
        grid=_row_grid(t, tm),
        in_specs=[pl.BlockSpec((tm, d), lambda i: (i, 0)),
                  pl.BlockSpec((None, 1, d), lambda i: (layer, 0, 0)),
                  pl.BlockSpec((None, ns, d), lambda i: (w_layer, col0 // ns, 0))],
        out_specs=[pl.BlockSpec((tm, d), lambda i: (i, 0)),
                   pl.BlockSpec((tm, ns), lambda i: (i, 0))],
        out_shape=[jax.ShapeDtypeStruct((t, d), BF16), jax.ShapeDtypeStruct((t, ns), F32)],
        compiler_params=_params("parallel"),
    )(h, g, w_t)


def _rmsnorm_router(h, g, layer, w_router, b_router, tm=ROW_TILE // 2):
    t, d = h.shape
    return pl.pallas_call(
        _norm_router_kernel,
        grid=_row_grid(t, tm),
        in_specs=[pl.BlockSpec((tm, d), lambda i: (i, 0)),
                  pl.BlockSpec((None, 1, d), lambda i: (layer, 0, 0)),
                  pl.BlockSpec((d, ROUTER_LANES), lambda i: (0, 0)),
                  pl.BlockSpec((1, ROUTER_LANES), lambda i: (0, 0))],
        out_specs=[pl.BlockSpec((tm, d // 2), lambda i: (i, 0)),
                   pl.BlockSpec((tm, MOE_TOP_K), lambda i: (i, 0)),
                   pl.BlockSpec((tm, MOE_TOP_K), lambda i: (i, 0))],
        out_shape=[jax.ShapeDtypeStruct((t, d // 2), jnp.uint32),
                   jax.ShapeDtypeStruct((t, MOE_TOP_K), jnp.int32),
                   jax.ShapeDtypeStruct((t, MOE_TOP_K), F32)],
        compiler_params=_params("parallel"),
    )(h, g, w_router, b_router)


def _mm_kernel(*refs, mode, tail, split_a, split_r, w_cols_major):
    a_ref, refs = refs[0], refs[1:]
    at_ref = a_ref
    if split_a:
        at_ref, refs = refs[0], refs[1:]
    w_ref, refs = refs[0], refs[1:]
    r_ref = rt_ref = None
    if mode != "plain":
        r_ref, refs = refs[0], refs[1:]
        rt_ref = r_ref
        if split_r:
            rt_ref, refs = refs[0], refs[1:]
    if mode == "ple":
        p_ref, wp_ref, o_ref, wb_ref, wpb_ref = refs
    else:
        o_ref, wb_ref = refs
    m = pl.program_id(1)

    @pl.when(m == 0)
    def _():
        if w_cols_major:
            _cast_cols_transposed(w_ref, wb_ref)
        else:
            _cast_rows(w_ref, wb_ref)
        if mode == "ple":
            _cast_rows(wp_ref, wpb_ref)

    def tile(src_ref, res_ref, rows):
        acc = jnp.dot(src_ref[rows, :], wb_ref[...], preferred_element_type=F32)
        if mode == "plain":
            o_ref[rows, :] = acc.astype(o_ref.dtype)
        elif mode == "res":
            o_ref[rows, :] = res_ref[rows, :] + acc
        else:
            pp = jnp.dot(p_ref[rows, :].astype(BF16), wpb_ref[...], preferred_element_type=F32)
            o_ref[rows, :] = res_ref[rows, :] + pp * _sigmoid(acc)

    if tail == 0:
        tile(a_ref, r_ref, slice(None))
    else:
        last = pl.num_programs(1) - 1
        pl.when(m < last)(lambda: tile(a_ref, r_ref, slice(None)))
        pl.when(m == last)(lambda: tile(at_ref, rt_ref, slice(0, tail)))


def _matmul(a, w, layer, n_cols, mode="plain", res=None, p=None, wp=None, a_tail=None, res_tail=None,
            w_cols_major=False, out_dtype=F32, tm=MM_ROW_TILE, tn=COL_TILE):
    k = a.shape[1]
    tm = min(tm, a.shape[0] // 256 * 256)
    split_a = a_tail is not None
    split_r = res_tail is not None
    if split_a:
        assert a.shape[0] % tm == 0 and a_tail.shape[0] < tm
        t, tail = a.shape[0] + a_tail.shape[0], a_tail.shape[0]
    else:
        t, tail = a.shape[0], a.shape[0] % tm
    if split_r:
        assert res.shape[0] == t - tail and res_tail.shape[0] == tail
    n_full = t // tm

    def full_rows(n, m):
        return jnp.minimum(m, n_full - 1)

    tn = min(tn, n_cols)
    assert n_cols % tn == 0
    grid = (n_cols // tn, pl.cdiv(t, tm))
    if split_a:
        in_specs = [pl.BlockSpec((tm, k), lambda n, m: (full_rows(n, m), 0)),
                    pl.BlockSpec((tail, k), lambda n, m: (0, 0))]
        args = [a, a_tail]
    else:
        in_specs = [pl.BlockSpec((tm, k), lambda n, m: (m, 0))]
        args = [a]
    if w_cols_major:
        in_specs.append(pl.BlockSpec((None, tn, k), lambda n, m: (layer, n, 0)))
    else:
        in_specs.append(pl.BlockSpec((None, k, tn), lambda n, m: (layer, 0, n)))
    scratch = [pltpu.VMEM((k, tn), BF16)]
    args.append(w)
    if mode in ("res", "ple"):
        if split_r:
            in_specs += [pl.BlockSpec((tm, tn), lambda n, m: (full_rows(n, m), n)),
                         pl.BlockSpec((tail, tn), lambda n, m: (0, n))]
            args += [res, res_tail]
        else:
            in_specs.append(pl.BlockSpec((tm, tn), lambda n, m: (m, n)))
            args.append(res)
    if mode == "ple":
        kp = p.shape[-1]
        in_specs += [pl.BlockSpec((None, tm, kp), lambda n, m: (layer, m, 0)),
                     pl.BlockSpec((None, kp, tn), lambda n, m: (layer, 0, n))]
        args += [p, wp]
        scratch.append(pltpu.VMEM((kp, tn), BF16))
    return pl.pallas_call(
        functools.partial(_mm_kernel, mode=mode, tail=tail, split_a=split_a, split_r=split_r,
                          w_cols_major=w_cols_major),
        grid=grid,
        in_specs=in_specs,
        out_specs=pl.BlockSpec((tm, tn), lambda n, m: (m, n)),
        out_shape=jax.ShapeDtypeStruct((t, n_cols), out_dtype),
        scratch_shapes=scratch,
        compiler_params=_params("arbitrary", "arbitrary"),
    )(*args)


def _softmax_sink_pv(s, sk, vb):
    m = jnp.maximum(jnp.max(s, axis=-1, keepdims=True), sk)
    p = jnp.exp(s - m)
    den = jnp.sum(p, axis=-1, keepdims=True) + jnp.exp(sk - m)
    return jnp.dot((p / den).astype(BF16), vb, preferred_element_type=F32)


_NT = (((1,), (1,)), ((), ()))


def _swa_prompt_kernel(sinks_ref, q_ref, kp_ref, kc_ref, vp_ref, vc_ref, o_ref, nk_ref, nv_ref, *, kvh, group):
    n = pl.program_id(1)
    w, kvd = kc_ref.shape
    hd = kvd // kvh
    kk = jnp.concatenate([kp_ref[...], kc_ref[...]], axis=0).astype(BF16)
    vv = jnp.concatenate([vp_ref[...], vc_ref[...]], axis=0).astype(BF16)
    qi = lax.broadcasted_iota(jnp.int32, (w, 2 * w), 0)
    kj = lax.broadcasted_iota(jnp.int32, (w, 2 * w), 1)
    mask = (kj > qi) & (kj <= qi + w) & (kj >= jnp.where(n > 0, 0, w))
    scale = hd ** -0.5
    for kv in range(kvh):
        kh = kk[:, kv * hd:(kv + 1) * hd]
        vh = vv[:, kv * hd:(kv + 1) * hd]
        for g in range(group):
            c0 = (kv * group + g) * hd
            qg = (q_ref[:, c0:c0 + hd] * scale).astype(BF16)
            s = lax.dot_general(qg, kh, _NT, preferred_element_type=F32)
            s = jnp.where(mask, s, -jnp.inf)
            o = _softmax_sink_pv(s, sinks_ref[kv * group + g], vh)
            o_ref[:, c0:c0 + hd] = o.astype(o_ref.dtype)

    @pl.when(n == pl.num_programs(1) - 1)
    def _():
        nk_ref[...] = kc_ref[...]
        nv_ref[...] = vc_ref[...]


def _swa_prompt(qkv, sinks, batch, seq, kvh):
    hd, w = SWA_HEAD_DIM, WINDOW
    kvd = kvh * hd
    qd = qkv.shape[1] - 2 * kvd
    group = qd // kvd
    nb = seq // w
    kcol = qd // kvd

    def cur(b, n):
        return b * nb + n

    def prev(b, n):
        return b * nb + jnp.maximum(n - 1, 0)

    last_spec = pl.BlockSpec((None, None, w, kvd), lambda b, n: (0, b, 0, 0))
    return pl.pallas_call(
        functools.partial(_swa_prompt_kernel, kvh=kvh, group=group),
        grid=(batch, nb),
        in_specs=[pl.BlockSpec(memory_space=pltpu.SMEM),
                  pl.BlockSpec((w, qd), lambda b, n: (cur(b, n), 0)),
                  pl.BlockSpec((w, kvd), lambda b, n: (prev(b, n), kcol)),
                  pl.BlockSpec((w, kvd), lambda b, n: (cur(b, n), kcol)),
                  pl.BlockSpec((w, kvd), lambda b, n: (prev(b, n), kcol + 1)),
                  pl.BlockSpec((w, kvd), lambda b, n: (cur(b, n), kcol + 1))],
        out_specs=[pl.BlockSpec((w, qd), lambda b, n: (cur(b, n), 0)), last_spec, last_spec],
        out_shape=[jax.ShapeDtypeStruct((batch * seq, qd), BF16),
                   jax.ShapeDtypeStruct((1, batch, w, kvd), F32),
                   jax.ShapeDtypeStruct((1, batch, w, kvd), F32)],
        compiler_params=_params("parallel", "arbitrary"),
    )(sinks, qkv, qkv, qkv, qkv, qkv)


def _swa_decode_kernel(sk_ref, q_ref, x_ref, kp_ref, vp_ref, nk_ref, nv_ref, o_ref, *, kvh, group):
    bb, w, kvd = kp_ref.shape
    hd = kvd // kvh
    nh = kvh * group
    qd = nh * hd
    scale = hd ** -0.5
    row = lax.broadcasted_iota(jnp.int32, (w, kvd), 0)
    head_kv = lax.broadcasted_iota(jnp.int32, (nh, hd), 0) // group
    sk = sk_ref[...]
    for bi in range(bb):
        x = x_ref[bi:bi + 1, :]
        kc = jnp.where(row == w - 1, x[:, qd:qd + kvd], pltpu.roll(kp_ref[bi], w - 1, axis=0))
        vc = jnp.where(row == w - 1, x[:, qd + kvd:], pltpu.roll(vp_ref[bi], w - 1, axis=0))
        nk_ref[bi] = kc
        nv_ref[bi] = vc
        s = lax.dot_general((q_ref[bi] * scale).astype(BF16), kc.astype(BF16), _NT, preferred_element_type=F32)
        ov = _softmax_sink_pv(s, sk, vc.astype(BF16))
        o = jnp.zeros((nh, hd), F32)
        for kv in range(kvh):
            o = jnp.where(head_kv == kv, ov[:, kv * hd:(kv + 1) * hd], o)
        o_ref[bi] = o


def _swa_decode(qkv, sinks, k_past, v_past, layer, row0, bb=8):
    nl, b, w, kvh, hd = k_past.shape
    kvd = kvh * hd
    qd = qkv.shape[1] - 2 * kvd
    group = qd // kvd
    nh = kvh * group
    bb = min(bb, b)
    blk0 = row0 // bb
    q = qkv[row0:row0 + b, :qd].reshape(b, nh, 1, hd)
    own = (jnp.arange(nh)[:, None] // group == jnp.arange(kvh)[None, :])[None, :, :, None]
    q_blocks = jnp.where(own, q, 0.0).reshape(b, nh, kvd)
    in_spec = pl.BlockSpec((None, bb, w, kvd), lambda i: (layer, i, 0, 0))
    out_spec = pl.BlockSpec((None, bb, w, kvd), lambda i: (0, i, 0, 0))
    nk, nv, o = pl.pallas_call(
        functools.partial(_swa_decode_kernel, kvh=kvh, group=group),
        grid=(b // bb,),
        in_specs=[pl.BlockSpec((nh, 1), lambda i: (0, 0)),
                  pl.BlockSpec((bb, nh, kvd), lambda i: (i, 0, 0)),
                  pl.BlockSpec((bb, qkv.shape[1]), lambda i: (blk0 + i, 0)),
                  in_spec, in_spec],
        out_specs=[out_spec, out_spec, pl.BlockSpec((bb, nh, hd), lambda i: (i, 0, 0))],
        out_shape=[jax.ShapeDtypeStruct((1, b, w, kvd), F32), jax.ShapeDtypeStruct((1, b, w, kvd), F32),
                   jax.ShapeDtypeStruct((b, nh, hd), F32)],
        compiler_params=_params("parallel"),
    )(sinks.reshape(nh, 1), q_blocks, qkv, k_past.reshape(nl, b, w, kvd), v_past.reshape(nl, b, w, kvd))
    return o.reshape(b, qd), nk.reshape(1, b, w, kvh, hd), nv.reshape(1, b, w, kvh, hd)


def _log_decay(gl, wgu, bg):
    x = jnp.dot(gl.astype(BF16), wgu.astype(BF16), preferred_element_type=F32) + bg
    return (jnp.minimum(x, 0.0) - jnp.log(1.0 + jnp.exp(-jnp.abs(x)))) * (1.0 / GLA_TAU)


def _gla_out(o, ng, r):
    return _rms(o, ng) * (r * _sigmoid(r))


def _block_ref_row(x, row, s):
    c, d = x.shape
    if 2 * s >= 8:
        nblk = c // (2 * s)
        ref = x.reshape(nblk, 2 * s, d)[:, s - 1:s, :]
        return jnp.broadcast_to(ref, (nblk, 2 * s, d)).reshape(c, d)
    pos = row & (2 * s - 1)
    out = x
    for delta in range(s - 1, -s - 1, -1):
        if delta != 0:
            out = jnp.where(pos == s - 1 - delta, pltpu.roll(x, (c - delta) % c, axis=0), out)
    return out


def _gla_prompt_kernel(q_ref, k_ref, v_ref, r_ref, gl_ref, wgu_ref, bg_ref, ng_ref, o_ref, s_ref):
    c, dkh = q_ref.shape

    @pl.when(pl.program_id(2) == 0)
    def _():
        s_ref[...] = jnp.zeros_like(s_ref)

    g = _log_decay(gl_ref[...], wgu_ref[...], bg_ref[...])
    row = lax.broadcasted_iota(jnp.int32, (c, dkh), 0)
    cum = g
    sh = 1
    while sh < c:
        cum = cum + jnp.where(row >= sh, pltpu.roll(cum, sh, axis=0), 0.0)
        sh *= 2
    q = q_ref[...] * (dkh ** -0.5)
    k = k_ref[...]
    vb = v_ref[...].astype(BF16)

    ri = lax.broadcasted_iota(jnp.int32, (c, c), 0)
    ci = lax.broadcasted_iota(jnp.int32, (c, c), 1)
    att = jnp.where(ri == ci, lax.dot_general(q.astype(BF16), k.astype(BF16), _NT, preferred_element_type=F32), 0.0)
    s = c // 2
    while s >= 1:
        cref = _block_ref_row(cum, row, s)
        e = jnp.exp(jnp.where((row & s) != 0, cum - cref, cref - cum))
        a_s = lax.dot_general((q * e).astype(BF16), (k * e).astype(BF16), _NT, preferred_element_type=F32)
        pair = (((ri ^ ci) >> (s.bit_length() - 1)) == 1) & ((ri & s) != 0)
        att = att + jnp.where(pair, a_s, 0.0)
        s //= 2

    st = s_ref[...]
    o = jnp.dot((q * jnp.exp(cum)).astype(BF16), st.astype(BF16), preferred_element_type=F32)
    o = o + jnp.dot(att.astype(BF16), vb, preferred_element_type=F32)
    o_ref[...] = _gla_out(o, ng_ref[...], r_ref[...]).astype(o_ref.dtype)

    last = cum[c - 1:c, :]
    kd_t = (k * jnp.exp(last - cum)).T.astype(BF16)
    e_col = jnp.exp(cum.T[:, c - 1:c])
    s_ref[...] = e_col * st + jnp.dot(kd_t, vb, preferred_element_type=F32)


def _gla_prompt(proj, gl, w_gate_up, b_gate, norm_g, batch, seq, t_out):
    h = GLA_HEADS
    dk = w_gate_up.shape[-1]
    dkh = dk // h
    dv = (proj.shape[1] - 2 * dk) // 2
    dvh = dv // h
    c = GLA_CHUNK
    nc = seq // c

    def rows(b, hh, i):
        return b * nc + i

    return pl.pallas_call(
        _gla_prompt_kernel,
        grid=(batch, h, nc),
        in_specs=[pl.BlockSpec((c, dkh), lambda b, hh, i: (rows(b, hh, i), hh)),
                  pl.BlockSpec((c, dkh), lambda b, hh, i: (rows(b, hh, i), h + hh)),
                  pl.BlockSpec((c, dvh), lambda b, hh, i: (rows(b, hh, i), 2 * dk // dvh + hh)),
                  pl.BlockSpec((c, dvh), lambda b, hh, i: (rows(b, hh, i), (2 * dk + dv) // dvh + hh)),
                  pl.BlockSpec((c, GLA_GATE_RANK), lambda b, hh, i: (rows(b, hh, i), 0)),
                  pl.BlockSpec((None, GLA_GATE_RANK, dkh), lambda b, hh, i: (0, 0, hh)),
                  pl.BlockSpec((1, dkh), lambda b, hh, i: (0, hh)),
                  pl.BlockSpec((1, dvh), lambda b, hh, i: (0, 0))],
        out_specs=[pl.BlockSpec((c, dvh), lambda b, hh, i: (rows(b, hh, i), hh)),
                   pl.BlockSpec((None, None, None, dkh, dvh), lambda b, hh, i: (0, b, hh, 0, 0))],
        out_shape=[jax.ShapeDtypeStruct((t_out, dv), BF16),
                   jax.ShapeDtypeStruct((1, batch, h, dkh, dvh), F32)],
        compiler_params=_params("parallel", "parallel", "arbitrary"),
    )(proj, proj, proj, proj, gl, w_gate_up, b_gate, norm_g)


def _gla_gate_kernel(gl_ref, wgu_ref, bg_ref, o_ref):
    o_ref[...] = _log_decay(gl_ref[...], wgu_ref[...], bg_ref[...])


def _gla_gate(gl, w_gate_up, b_gate, row0, rows):
    dk = w_gate_up.shape[-1]
    return pl.pallas_call(
        _gla_gate_kernel,
        grid=(1,),
        in_specs=[pl.BlockSpec((rows, GLA_GATE_RANK), lambda i: (row0 // rows, 0)),
                  pl.BlockSpec((None, GLA_GATE_RANK, dk), lambda i: (0, 0, 0)),
                  pl.BlockSpec((1, dk), lambda i: (0, 0))],
        out_specs=pl.BlockSpec((rows, dk), lambda i: (0, 0)),
        out_shape=jax.ShapeDtypeStruct((rows, dk), F32),
        compiler_params=_params("arbitrary"),
    )(gl, w_gate_up, b_gate)


def _gla_decode_kernel(q_ref, k_ref, g_ref, v_ref, r_ref, ng_ref, s_ref, o_ref, ns_ref):
    dkh = q_ref.shape[-1]
    rows = jnp.concatenate([q_ref[...], k_ref[...], g_ref[...], jnp.zeros((128 - 3, dkh), F32)], axis=0)
    cols = rows.T
    q_col, k_col, g_col = cols[:, 0:1], cols[:, 1:2], cols[:, 2:3]
    sn = jnp.exp(g_col) * s_ref[...] + k_col * v_ref[...]
    ns_ref[...] = sn
    o = jnp.sum((q_col * (dkh ** -0.5)) * sn, axis=0, keepdims=True)
    o_ref[...] = _gla_out(o, ng_ref[...], r_ref[...])


def _gla_decode(proj_s, log_a, norm_g, state, layer):
    b, h, dkh, dvh = state.shape[1:]
    dk, dv = h * dkh, h * dvh
    st_spec = pl.BlockSpec((None, None, None, dkh, dvh), lambda i, hh: (layer, i, hh, 0, 0))
    st_out_spec = pl.BlockSpec((None, None, None, dkh, dvh), lambda i, hh: (0, i, hh, 0, 0))
    return pl.pallas_call(
        _gla_decode_kernel,
        grid=(b, h),
        in_specs=[pl.BlockSpec((None, 1, dkh), lambda i, hh: (i, 0, hh)),
                  pl.BlockSpec((None, 1, dkh), lambda i, hh: (i, 0, h + hh)),
                  pl.BlockSpec((None, 1, dkh), lambda i, hh: (i, 0, hh)),
                  pl.BlockSpec((None, 1, dvh), lambda i, hh: (i, 0, 2 * dk // dvh + hh)),
                  pl.BlockSpec((None, 1, dvh), lambda i, hh: (i, 0, (2 * dk + dv) // dvh + hh)),
                  pl.BlockSpec((1, dvh), lambda i, hh: (0, 0)),
                  st_spec],
        out_specs=[pl.BlockSpec((None, 1, dvh), lambda i, hh: (i, 0, hh)), st_out_spec],
        out_shape=[jax.ShapeDtypeStruct((b, 1, dv), F32), jax.ShapeDtypeStruct((1, b, h, dkh, dvh), F32)],
        compiler_params=_params("parallel", "parallel"),
    )(proj_s, proj_s, log_a.reshape(b, 1, dk), proj_s, proj_s, norm_g, state)


def _pack_bf16_halves(y):
    half = y.shape[1] // 2
    bits = lax.bitcast_convert_type(y.astype(BF16).astype(F32), jnp.uint32)
    return (bits[:, :half] >> 16) | (bits[:, half:] & jnp.uint32(0xFFFF0000))


def _unpack_bf16_halves(u):
    lo = lax.bitcast_convert_type(u << 16, F32)
    hi = lax.bitcast_convert_type(u & jnp.uint32(0xFFFF0000), F32)
    return jnp.concatenate([lo, hi], axis=1).astype(BF16)


def _for_rows(n, fn, unroll=8, with_lane=False):
    def group(j, c):
        for u in range(unroll):
            fn(j * unroll + u, u) if with_lane else fn(j * unroll + u)
        return c

    def single(r, c):
        fn(r, 0) if with_lane else fn(r)
        return c

    full = n // unroll
    lax.fori_loop(0, full, group, 0)
    lax.fori_loop(full * unroll, n, single, 0)


def _moe_kernel(ie_ref, ir_ref, in_ref, tok_ref, dst_ref, x_hbm, wg_ref, wu_ref, wd_ref, y_hbm,
                xbuf, ybuf, gsem, ssem):
    i = pl.program_id(0)
    f = pl.program_id(1)
    ni = pl.num_programs(0)
    nf = pl.num_programs(1)
    n = in_ref[i]
    slot = i % 2
    rb = ybuf.shape[0]
    sizes = sorted({min(m, rb) for m in MOE_ROW_SIZES})

    def gather(item, sl, r):
        src = x_hbm.at[pl.ds(tok_ref[ir_ref[item] + r], 1)]
        return pltpu.make_async_copy(src, xbuf.at[sl, pl.ds(r, 1)], gsem.at[sl])

    def scatter(item, r):
        dst = y_hbm.at[pl.ds(dst_ref[ir_ref[item] + r], 1)]
        return pltpu.make_async_copy(ybuf.at[pl.ds(r, 1)], dst, ssem)

    @pl.when(f == 0)
    def _():
        @pl.when(i == 0)
        def _():
            xbuf[...] = jnp.zeros_like(xbuf)
            ybuf[...] = jnp.zeros_like(ybuf)
            _for_rows(n, lambda r: gather(0, 0, r).start())

        nxt = jnp.minimum(i + 1, ni - 1)
        n_nxt = jnp.where(i + 1 < ni, in_ref[nxt], 0)
        _for_rows(n_nxt, lambda r: gather(nxt, 1 - slot, r).start())
        _for_rows(n, lambda r: gather(i, slot, r).wait())
        prv = jnp.maximum(i - 1, 0)
        n_prv = jnp.where(i > 0, in_ref[prv], 0)
        _for_rows(n_prv, lambda r: scatter(prv, r).wait())

    lo = 0
    for m in sizes:
        @pl.when((n > lo) & (n <= m))
        def _(m=m):
            xs = _unpack_bf16_halves(xbuf[slot, :m, :])
            hg = jnp.dot(xs, wg_ref[...].astype(BF16), preferred_element_type=F32)
            hu = jnp.dot(xs, wu_ref[...].astype(BF16), preferred_element_type=F32)
            act = (hg * _sigmoid(hg) * hu).astype(BF16)
            y = jnp.dot(act, wd_ref[...].astype(BF16), preferred_element_type=F32)
            ybuf[:m, :] = y + jnp.where(f > 0, ybuf[:m, :], 0.0)
        lo = m

    @pl.when(f == nf - 1)
    def _():
        _for_rows(n, lambda r, u: scatter(i, r).start(priority=u % 2), with_lane=True)

        @pl.when(i == ni - 1)
        def _():
            _for_rows(n, lambda r: scatter(i, r).wait())


def _moe_experts(xn, eid, w_gate, w_up, w_down, layer):
    t, d = xn.shape[0], xn.shape[1] * 2
    n_exp, _, dff = w_gate.shape[1:]
    a = t * MOE_TOP_K
    rb = min(MOE_ROW_BLOCK, -(-a // 8) * 8)
    fc = min(MOE_FF_TILE, dff)
    nf = dff // fc
    n_items = n_exp + a // rb

    flat = eid.reshape(-1)
    order = jnp.argsort(flat, stable=True).astype(jnp.int32)
    counts = jnp.sum((flat[:, None] == jnp.arange(n_exp, dtype=jnp.int32)[None, :]).astype(jnp.int32), axis=0)
    starts = jnp.cumsum(counts) - counts
    per_e = (counts + rb - 1) // rb
    item_end = jnp.cumsum(per_e)
    item_start = item_end - per_e
    ii = jnp.arange(n_items, dtype=jnp.int32)
    valid = ii < item_end[-1]
    e_of = jnp.minimum(jnp.searchsorted(item_end, ii, side="right").astype(jnp.int32), n_exp - 1)
    local = ii - item_start[e_of]
    item_rows = jnp.where(valid, jnp.clip(counts[e_of] - local * rb, 0, rb), 0).astype(jnp.int32)
    item_row0 = jnp.where(valid, starts[e_of] + local * rb, 0).astype(jnp.int32)
    e_last = jnp.max(jnp.where(valid, e_of, 0))
    item_e = jnp.where(valid, e_of, e_last).astype(jnp.int32)
    tok = order // MOE_TOP_K
    dst = (order % MOE_TOP_K) * t + tok

    def w_in_map(i, f, ie, ir, inn, tk, ds_):
        return (layer, ie[i], 0, jnp.where(inn[i] > 0, f, nf - 1))

    def w_out_map(i, f, ie, ir, inn, tk, ds_):
        return (layer, ie[i], jnp.where(inn[i] > 0, f, nf - 1), 0)

    grid_spec = pltpu.PrefetchScalarGridSpec(
        num_scalar_prefetch=5,
        grid=(n_items, nf),
        in_specs=[pl.BlockSpec(memory_space=pl.ANY),
                  pl.BlockSpec((None, None, d, fc), w_in_map),
                  pl.BlockSpec((None, None, d, fc), w_in_map),
                  pl.BlockSpec((None, None, fc, d), w_out_map)],
        out_specs=pl.BlockSpec(memory_space=pl.ANY),
        scratch_shapes=[pltpu.VMEM((2, rb, d // 2), jnp.uint32), pltpu.VMEM((rb, d), F32),
                        pltpu.SemaphoreType.DMA((2,)), pltpu.SemaphoreType.DMA(())],
    )
    y = pl.pallas_call(
        _moe_kernel,
        grid_spec=grid_spec,
        out_shape=jax.ShapeDtypeStruct((a, d), F32),
        compiler_params=_params("arbitrary", "arbitrary"),
    )(item_e, item_row0, item_rows, tok, dst, xn, w_gate, w_up, w_down)
    return y.reshape(MOE_TOP_K, t, d)


def _combine_norm_kernel(h_ref, y0_ref, y1_ref, gate_ref, g_ref, ho_ref, a_ref):
    h = h_ref[...] + (y0_ref[...] * gate_ref[:, 0:1] + y1_ref[...] * gate_ref[:, 1:2])
    ho_ref[...] = h
    a_ref[...] = _rms(h, g_ref[...]).astype(a_ref.dtype)


def _combine_norm(h, y2, gates, g, layer, tm=ROW_TILE // 2):
    t, d = h.shape
    return pl.pallas_call(
        _combine_norm_kernel,
        grid=_row_grid(t, tm),
        in_specs=[pl.BlockSpec((tm, d), lambda i: (i, 0)),
                  pl.BlockSpec((None, tm, d), lambda i: (0, i, 0)),
                  pl.BlockSpec((None, tm, d), lambda i: (1, i, 0)),
                  pl.BlockSpec((tm, MOE_TOP_K), lambda i: (i, 0)),
                  pl.BlockSpec((None, 1, d), lambda i: (layer, 0, 0))],
        out_specs=[pl.BlockSpec((tm, d), lambda i: (i, 0)), pl.BlockSpec((tm, d), lambda i: (i, 0))],
        out_shape=[jax.ShapeDtypeStruct((t, d), F32), jax.ShapeDtypeStruct((t, d), BF16)],
        compiler_params=_params("parallel"),
    )(h, y2, y2, gates, g)


def kernel(x_prompt, x_sample, cache_swa_k, cache_swa_v, state_gla, p_prompt, p_sample, ln_mix, ln_ffn, ln_ple, ln_final, swa_w_in, swa_sinks, swa_w_out, gla_w_in, gla_w_gate_up, gla_b_gate, gla_norm, gla_w_out, moe_w_group, moe_b_group, moe_w_expert, moe_b_expert, moe_w_gate, moe_w_up, moe_w_down, ple_w_proj, ple_w_gate):
    batch, seq, d = x_prompt.shape
    dec = x_sample.shape[0]
    assert x_sample.shape[1] == 1
    tp = batch * seq
    t = tp + dec
    depth = ln_mix.shape[0]
    kvh = cache_swa_k.shape[3]
    kvd = kvh * SWA_HEAD_DIM
    dk = gla_w_gate_up.shape[-1]
    dv = (gla_w_in.shape[-1] - GLA_GATE_RANK - 2 * dk) // 2

    x_p, x_s = x_prompt.reshape(tp, d), x_sample.reshape(dec, d)
    h = None
    p_all = jnp.concatenate([p_prompt.reshape(depth, tp, -1), p_sample.reshape(depth, dec, -1)], axis=1)
    pad = ROUTER_LANES - MOE_GROUPS - MOE_EXPERTS
    w_router = jnp.concatenate([moe_w_group, moe_w_expert, jnp.zeros((depth, d, pad), F32)], axis=-1)
    b_router = jnp.concatenate([moe_b_group, moe_b_expert, jnp.zeros((depth, pad), F32)], axis=-1)
    ln_mix, ln_ffn, ln_ple = (g.reshape(depth, 1, d) for g in (ln_mix, ln_ffn, ln_ple))

    new_k_p, new_v_p, new_s_p, new_k_s, new_v_s, new_s_s = [], [], [], [], [], []
    for i in range(depth):
        j = i // 2
        if i % 2 == 0:
            if h is None:
                qkv = _matmul(_rmsnorm(x_p, ln_mix, i), swa_w_in, j, swa_w_in.shape[-1],
                              a_tail=_rmsnorm(x_s, ln_mix, i))
            else:
                qkv = _matmul(_rmsnorm(h, ln_mix, i), swa_w_in, j, swa_w_in.shape[-1])
            o, nk_p, nv_p = _swa_prompt(qkv, swa_sinks[j], batch, seq, kvh)
            o_s, nk, nv = _swa_decode(qkv, swa_sinks[j], cache_swa_k, cache_swa_v, j, tp)
            new_k_p.append(nk_p.reshape(1, batch, WINDOW, kvh, SWA_HEAD_DIM))
            new_v_p.append(nv_p.reshape(1, batch, WINDOW, kvh, SWA_HEAD_DIM))
            new_k_s.append(nk)
            new_v_s.append(nv)
            if h is None:
                h = _matmul(o, swa_w_out, j, d, mode="res", res=x_p, res_tail=x_s, a_tail=o_s.astype(BF16))
            else:
                h = _matmul(o, swa_w_out, j, d, mode="res", res=h, a_tail=o_s.astype(BF16))
        else:
            w_in_t = jnp.swapaxes(gla_w_in, 1, 2)
            a, gl = _rmsnorm_proj(h, ln_mix, i, w_in_t, j, 2 * dk + 2 * dv, GLA_GATE_RANK)
            proj = _matmul(a, w_in_t, j, 2 * dk + 2 * dv, w_cols_major=True)
            o, s_p = _gla_prompt(proj, gl, gla_w_gate_up[j:j + 1], gla_b_gate[j:j + 1], gla_norm[j:j + 1], batch, seq, tp)
            log_a = _gla_gate(gl, gla_w_gate_up[j:j + 1], gla_b_gate[j:j + 1], tp, dec)
            o_s, s_s = _gla_decode(proj[tp:].reshape(dec, 1, -1), log_a, gla_norm[j:j + 1], state_gla, j)
            new_s_p.append(s_p)
            new_s_s.append(s_s)
            h = _matmul(o, gla_w_out, j, d, mode="res", res=h, a_tail=o_s.reshape(dec, dv).astype(BF16))
        xn, eid, gates = _rmsnorm_router(h, ln_ffn, i, w_router[i], b_router[i:i + 1])
        y2 = _moe_experts(xn, eid, moe_w_gate, moe_w_up, moe_w_down, i)
        h, a3 = _combine_norm(h, y2, gates, ln_ple, i)
        h = _matmul(a3, ple_w_gate, i, d, mode="ple", res=h, p=p_all, wp=ple_w_proj)

    g_fin = ln_final.reshape(1, 1, d)
    y_p = _rmsnorm(h, g_fin, 0, out_dtype=F32, rows=tp)
    y_s = _rmsnorm(h, g_fin, 0, out_dtype=F32, rows=dec, row_block0=tp // dec, tm=dec)
    return (y_p.reshape(batch, seq, d), y_s.reshape(dec, 1, d),
            *(per_layer[0] if len(per_layer) == 1 else jnp.concatenate(per_layer)
              for per_layer in (new_k_p, new_v_p, new_s_p, new_k_s, new_v_s, new_s_s)))
```
